```python
import math
import jax
import jax.numpy as jnp
from jax import lax
import numpy as np

D_MODEL = 1024
BATCH = 4
SEQ = 8192
DEPTH = 1
DEC_BATCH = 128
DEC_SEQ = 4
PAST_LEN = 16384
PAGE_SIZE = 128

MLA_HEADS = 8
MLA_NOPE = 64
MLA_ROPE = 32
MLA_QK = MLA_NOPE + MLA_ROPE
MLA_VDIM = 64
Q_LORA = 256
KV_LORA = 128
MLA_ROW = KV_LORA + MLA_ROPE
ROPE_BASE = 10000.0
NSA_HEADS = 8
NSA_GROUPS = 2
NSA_HPG = NSA_HEADS // NSA_GROUPS
NSA_DH = 64
CMP_BLOCK = 64
CMP_HID = 128
N_SEL = 16
WINDOW = 512
BLOCKS_PER_PAGE = PAGE_SIZE // CMP_BLOCK
N_BUCKETS = 32
MAX_DISTANCE = 128
N_GROUPS = 4
EXPERTS_PER_GROUP = 8
N_EXPERTS = N_GROUPS * EXPERTS_PER_GROUP
TOP_K_IN_GROUP = 2
D_EXPERT = 256

Q_BLOCK = 128
EPS = 1e-6
NEG = -1e30
MLA_SCALE = MLA_QK ** -0.5
NSA_SCALE = NSA_DH ** -0.5
IN_SIZES = (Q_LORA, KV_LORA, MLA_ROPE, NSA_HEADS * NSA_DH, 6 * NSA_GROUPS * NSA_DH, 3 * NSA_HEADS, 2 * D_MODEL)
IN_TOTAL = sum(IN_SIZES)

kernel_name = 'mla_nsa_hier_moe_step'


def rmsnorm(x, g):
    xf = x.astype(jnp.float32)
    y = xf * lax.rsqrt(jnp.mean(xf * xf, axis=-1, keepdims=True) + EPS)
    return (y * g.astype(jnp.float32)).astype(x.dtype)


def masked_softmax(s, mask):
    s = jnp.where(mask, s.astype(jnp.float32), NEG)
    m = jnp.max(s, axis=-1, keepdims=True)
    p = jnp.exp(s - m) * mask
    return p / jnp.maximum(jnp.sum(p, axis=-1, keepdims=True), 1e-30)


def rope(x, pos):
    half = MLA_ROPE // 2
    inv = ROPE_BASE ** (-jnp.arange(half, dtype=jnp.float32) / half)
    ang = pos.astype(jnp.float32)[:, None] * inv
    ang = ang.reshape((ang.shape[0],) + (1,) * (x.ndim - 3) + (half,))
    cos, sin = jnp.cos(ang), jnp.sin(ang)
    xf = x.astype(jnp.float32)
    x1, x2 = xf[..., :half], xf[..., half:]
    return jnp.concatenate([x1 * cos - x2 * sin, x1 * sin + x2 * cos], axis=-1).astype(x.dtype)


def t5_bucket(dist):
    n = jnp.maximum(dist, 0)
    max_exact = N_BUCKETS // 2
    log_ratio = jnp.log(jnp.maximum(n, 1).astype(jnp.float32) / max_exact) / math.log(MAX_DISTANCE / max_exact)
    large = max_exact + (log_ratio * (N_BUCKETS - max_exact)).astype(jnp.int32)
    return jnp.where(n < max_exact, n, jnp.minimum(large, N_BUCKETS - 1))


def split_in(z):
    offs = np.cumsum(IN_SIZES)[:-1].tolist()
    return jnp.split(z, offs, axis=-1)


def mla_tokens(c_q_raw, c_kv_raw, kpe_raw, pos, p):
    lead = c_q_raw.shape[:-1]
    q = (rmsnorm(c_q_raw, p['g_q_lat']) @ p['w_q_b']).reshape(lead + (MLA_HEADS, MLA_QK))
    q = jnp.concatenate([q[..., :MLA_NOPE], rope(q[..., MLA_NOPE:], pos)], axis=-1)
    q = rmsnorm(q, p['g_mla_q'])
    rows = jnp.concatenate([rmsnorm(c_kv_raw, p['g_kv_lat']), rope(kpe_raw, pos)], axis=-1)
    return q, rows


def mla_keys(rows, p):
    lead = rows.shape[:-1]
    kv = (rows[..., :KV_LORA] @ p['w_ukv']).reshape(lead + (MLA_HEADS, MLA_NOPE + MLA_VDIM))
    k_pe = jnp.broadcast_to(rows[..., None, KV_LORA:], lead + (MLA_HEADS, MLA_ROPE))
    k = rmsnorm(jnp.concatenate([kv[..., :MLA_NOPE], k_pe], axis=-1), p['g_mla_k'])
    return k, kv[..., MLA_NOPE:]


def mla_attend(q, k, v, qpos, kpos):
    s = jnp.einsum('bqhd,bkhd->bhqk', q, k).astype(jnp.float32) * MLA_SCALE
    pr = masked_softmax(s, kpos[None, :] <= qpos[:, None])
    o = jnp.einsum('bhqk,bkhd->bqhd', pr.astype(v.dtype), v)
    return o.reshape(o.shape[:2] + (MLA_HEADS * MLA_VDIM,))


def nsa_tokens(nq_raw, nkv_raw, ngate_raw, p):
    lead = nq_raw.shape[:-1]
    q = rmsnorm(nq_raw.reshape(lead + (NSA_GROUPS, NSA_HPG, NSA_DH)), p['g_nsa_q'])
    kv = nkv_raw.reshape(lead + (6, NSA_GROUPS, NSA_DH))
    gates = jax.nn.sigmoid(ngate_raw.reshape(lead + (3, NSA_GROUPS, NSA_HPG)))[..., None]
    return q, kv, gates


def compress(rows, pe, w1, b1, w2):
    b_, length = rows.shape[:2]
    nb = length // CMP_BLOCK
    blk = rows.reshape(b_, nb, CMP_BLOCK, NSA_GROUPS, NSA_DH) + pe[:, None, :]
    flat = blk.transpose(0, 1, 3, 2, 4).reshape(b_, nb, NSA_GROUPS, CMP_BLOCK * NSA_DH)
    return jax.nn.gelu(flat @ w1 + b1) @ w2


def shared_key_attention(q, k, v, qpos, kpos, table_g, window):
    s = jnp.einsum('bqgrd,bkgd->bgrqk', q, k).astype(jnp.float32) * NSA_SCALE
    dist = qpos[:, None] - kpos[None, :]
    s = s + jnp.transpose(table_g[t5_bucket(dist)], (2, 3, 0, 1)).astype(jnp.float32)
    mask = (dist >= 0) & (kpos[None, :] >= 0)
    if window is not None:
        mask = mask & (dist < window)
    pr = masked_softmax(s, mask)
    o = jnp.einsum('bgrqk,bkgd->bqgrd', pr.astype(v.dtype), v)
    return o, pr


def gathered_attention(q, k, v, qpos, kpos, kvalid, table_g):
    s = jnp.einsum('btgrd,btgkd->btgrk', q, k).astype(jnp.float32) * NSA_SCALE
    dist = qpos[None, :, None, None] - kpos
    gi = jnp.arange(NSA_GROUPS)[None, None, :, None]
    bias = table_g[t5_bucket(dist), gi]
    s = s + jnp.swapaxes(bias, -1, -2).astype(jnp.float32)
    mask = ((dist >= 0) & kvalid)[:, :, :, None, :]
    pr = masked_softmax(s, mask)
    return jnp.einsum('btgrk,btgkd->btgrd', pr.astype(v.dtype), v)


def select_blocks(imp, cand):
    n_top = N_SEL - 1
    nb = imp.shape[-1]
    scores = jnp.where(cand, imp, -1.0)
    if nb < n_top:
        scores = jnp.pad(scores, ((0, 0), (0, 0), (0, 0), (0, n_top - nb)), constant_values=-1.0)
    vals, idx = lax.top_k(scores, n_top)
    valid = vals >= 0.0
    idx = jnp.minimum(idx, nb - 1)
    return jnp.transpose(idx, (0, 2, 1, 3)), jnp.transpose(valid, (0, 2, 1, 3))


def nsa_combine(gates, o_cmp, o_slc, o_win):
    o = gates[:, :, 0] * o_cmp + gates[:, :, 1] * o_slc + gates[:, :, 2] * o_win
    return o.reshape(o.shape[:2] + (NSA_HEADS * NSA_DH,))


def hier_moe(h, p):
    t = h.reshape(-1, D_MODEL)
    g_logits = (t @ p['w_router_group'] + p['b_router_group']).astype(jnp.float32)
    g_sel = jnp.argmax(g_logits, axis=-1)
    g_w = jnp.take_along_axis(jax.nn.softmax(g_logits, axis=-1), g_sel[:, None], axis=-1)
    e_logits = (t @ p['w_router_expert'] + p['b_router_expert']).astype(jnp.float32)
    e_logits = e_logits.reshape(-1, N_GROUPS, EXPERTS_PER_GROUP)
    e_in = jnp.take_along_axis(e_logits, g_sel[:, None, None], axis=1)[:, 0]
    top_w, top_i = lax.top_k(jax.nn.softmax(e_in, axis=-1), TOP_K_IN_GROUP)
    top_w = g_w * top_w / jnp.sum(top_w, axis=-1, keepdims=True)
    eid = g_sel[:, None] * EXPERTS_PER_GROUP + top_i
    combine = jnp.sum(jax.nn.one_hot(eid, N_EXPERTS, dtype=jnp.float32) * top_w[..., None], axis=1).astype(t.dtype)
    out = jnp.zeros_like(t)
    for e in range(N_EXPERTS):
        hid = jax.nn.silu(t @ p['w_e_gate'][e]) * (t @ p['w_e_up'][e])
        out = out + combine[:, e:e + 1] * (hid @ p['w_e_down'][e])
    return out.reshape(h.shape)


def project_tokens(x, pos, p):
    h = rmsnorm(x, p['g_attn'])
    c_q, c_kv, kpe, nq, nkv, ngate, mgate = split_in(h @ p['w_in'])
    q_mla, mla_rows = mla_tokens(c_q, c_kv, kpe, pos, p)
    q_nsa, nsa_kv, nsa_gates = nsa_tokens(nq, nkv, ngate, p)
    return q_mla, mla_rows, q_nsa, nsa_kv, nsa_gates, mgate


def merge_and_ffn(x, o_mla, o_nsa, mgate, p):
    g = jax.nn.sigmoid(mgate)
    y = g[..., :D_MODEL] * (o_mla @ p['w_up_mla']) + g[..., D_MODEL:] * (o_nsa @ p['w_up_nsa'])
    x = x + y @ p['w_out']
    return x + hier_moe(rmsnorm(x, p['g_ffn']), p)


def prompt_layer(x, p, table_g):
    b_, s_, _ = x.shape
    pos = jnp.arange(s_, dtype=jnp.int32)
    nqb = s_ // Q_BLOCK
    q_mla, mla_rows, q, kv, gates, mgate = project_tokens(x, pos, p)

    def to_blocks(a):
        return jnp.moveaxis(a.reshape((b_, nqb, Q_BLOCK) + a.shape[2:]), 1, 0)

    def from_blocks(a):
        return jnp.moveaxis(a, 0, 1).reshape((b_, s_) + a.shape[3:])

    qpos_blocks = pos.reshape(nqb, Q_BLOCK)
    k_mla, v_mla = mla_keys(mla_rows, p)

    def mla_block(args):
        q_b, qpos = args
        return mla_attend(q_b, k_mla, v_mla, qpos, pos)
    o_mla = from_blocks(lax.map(mla_block, (to_blocks(q_mla), qpos_blocks)))

    nb = s_ // CMP_BLOCK
    kc = rmsnorm(compress(kv[:, :, 0], p['pe_cmp_k'], p['w_cmp_k1'], p['b_cmp_k1'], p['w_cmp_k2']), p['g_k_cmp'])
    vc = compress(kv[:, :, 1], p['pe_cmp_v'], p['w_cmp_v1'], p['b_cmp_v1'], p['w_cmp_v2'])
    kpos_c = jnp.arange(nb, dtype=jnp.int32) * CMP_BLOCK + CMP_BLOCK - 1
    o_cmp, p_cmp = shared_key_attention(q, kc, vc, pos, kpos_c, table_g, None)

    cur_blk = pos // CMP_BLOCK
    cand = jnp.arange(nb)[None, :] < cur_blk[:, None]
    idx, valid = select_blocks(jnp.sum(p_cmp, axis=2), cand)
    idx = jnp.concatenate([idx, jnp.broadcast_to(cur_blk[None, :, None, None], (b_, s_, NSA_GROUPS, 1))], axis=-1)
    valid = jnp.concatenate([valid, jnp.ones((b_, s_, NSA_GROUPS, 1), dtype=bool)], axis=-1)
    ks_g = rmsnorm(kv[:, :, 2], p['g_k_slc']).reshape(b_, nb, CMP_BLOCK, NSA_GROUPS, NSA_DH).transpose(0, 3, 1, 2, 4)
    vs_g = kv[:, :, 3].reshape(b_, nb, CMP_BLOCK, NSA_GROUPS, NSA_DH).transpose(0, 3, 1, 2, 4)
    bi = jnp.arange(b_)[:, None, None, None]
    gi = jnp.arange(NSA_GROUPS)[None, None, :, None]

    def sel_block(args):
        q_b, idx_b, valid_b, qpos = args
        k_b = ks_g[bi, gi, idx_b].reshape(b_, Q_BLOCK, NSA_GROUPS, N_SEL * CMP_BLOCK, NSA_DH)
        v_b = vs_g[bi, gi, idx_b].reshape(b_, Q_BLOCK, NSA_GROUPS, N_SEL * CMP_BLOCK, NSA_DH)
        kpos = (idx_b[..., None] * CMP_BLOCK + jnp.arange(CMP_BLOCK)).reshape(b_, Q_BLOCK, NSA_GROUPS, -1)
        return gathered_attention(q_b, k_b, v_b, qpos, kpos, jnp.repeat(valid_b, CMP_BLOCK, axis=-1), table_g)
    o_slc = from_blocks(lax.map(sel_block, (to_blocks(q), to_blocks(idx), to_blocks(valid), qpos_blocks)))

    pad = ((0, 0), (WINDOW, 0), (0, 0), (0, 0))
    kw = jnp.pad(rmsnorm(kv[:, :, 4], p['g_k_win']), pad)
    vw = jnp.pad(kv[:, :, 5], pad)

    def win_block(args):
        q_b, n = args
        start = n * Q_BLOCK
        k_b = lax.dynamic_slice_in_dim(kw, start, Q_BLOCK + WINDOW, axis=1)
        v_b = lax.dynamic_slice_in_dim(vw, start, Q_BLOCK + WINDOW, axis=1)
        kpos = start - WINDOW + jnp.arange(Q_BLOCK + WINDOW, dtype=jnp.int32)
        o, _ = shared_key_attention(q_b, k_b, v_b, start + jnp.arange(Q_BLOCK, dtype=jnp.int32), kpos, table_g, WINDOW)
        return o
    o_win = from_blocks(lax.map(win_block, (to_blocks(q), jnp.arange(nqb, dtype=jnp.int32))))

    y = merge_and_ffn(x, o_mla, nsa_combine(gates, o_cmp, o_slc, o_win), mgate, p)
    wbuf = min(WINDOW, s_)
    return y, mla_rows, kv[:, :, :4], kv[:, s_ - wbuf:, 4:]


def sample_layer(x, cache_mla, cache_nsa, win_buf, page_table, l, p, table_g):
    b_, t_, _ = x.shape
    past = page_table.shape[1] * PAGE_SIZE
    wbuf = win_buf.shape[1]
    pos = past + jnp.arange(t_, dtype=jnp.int32)
    q_mla, mla_rows, q, kv, gates, mgate = project_tokens(x, pos, p)

    kpos_all = jnp.arange(past + t_, dtype=jnp.int32)

    def mla_seq(args):
        q_b, rows_b, pt_b = args
        rows = jnp.concatenate([cache_mla[l, pt_b].reshape(past, MLA_ROW), rows_b], axis=0)
        k, v = mla_keys(rows, p)
        return mla_attend(q_b[None], k[None], v[None], pos, kpos_all)[0]
    o_mla = lax.map(mla_seq, (q_mla, mla_rows, page_table))

    def cmp_seq(pt_b):
        raw = cache_nsa[l, pt_b, :, :2].reshape(1, past, 2, NSA_GROUPS, NSA_DH)
        kc_b = compress(raw[:, :, 0], p['pe_cmp_k'], p['w_cmp_k1'], p['b_cmp_k1'], p['w_cmp_k2'])
        vc_b = compress(raw[:, :, 1], p['pe_cmp_v'], p['w_cmp_v1'], p['b_cmp_v1'], p['w_cmp_v2'])
        return kc_b[0], vc_b[0]
    kc, vc = lax.map(cmp_seq, page_table)
    kc = rmsnorm(kc, p['g_k_cmp'])
    nb = past // CMP_BLOCK
    kpos_c = jnp.arange(nb, dtype=jnp.int32) * CMP_BLOCK + CMP_BLOCK - 1
    o_cmp, p_cmp = shared_key_attention(q, kc, vc, pos, kpos_c, table_g, None)

    cand = jnp.arange(nb)[None, :] < (pos // CMP_BLOCK)[:, None]
    idx, valid = select_blocks(jnp.sum(p_cmp, axis=2), cand)
    bi = jnp.arange(b_)[:, None, None, None]
    phys = page_table[bi, idx // BLOCKS_PER_PAGE][..., None]
    rows_i = (idx % BLOCKS_PER_PAGE)[..., None] * CMP_BLOCK + jnp.arange(CMP_BLOCK)
    gi = jnp.arange(NSA_GROUPS)[None, None, :, None, None]
    k_old = rmsnorm(cache_nsa[l, phys, rows_i, 2, gi], p['g_k_slc']).reshape(b_, t_, NSA_GROUPS, -1, NSA_DH)
    v_old = cache_nsa[l, phys, rows_i, 3, gi].reshape(b_, t_, NSA_GROUPS, -1, NSA_DH)
    kpos_old = (idx[..., None] * CMP_BLOCK + jnp.arange(CMP_BLOCK)).reshape(b_, t_, NSA_GROUPS, -1)
    valid_old = jnp.repeat(valid, CMP_BLOCK, axis=-1)
    new_shape = (b_, t_, NSA_GROUPS, t_, NSA_DH)
    k_new = jnp.broadcast_to(jnp.swapaxes(rmsnorm(kv[:, :, 2], p['g_k_slc']), 1, 2)[:, None], new_shape)
    v_new = jnp.broadcast_to(jnp.swapaxes(kv[:, :, 3], 1, 2)[:, None], new_shape)
    k_s = jnp.concatenate([k_old, k_new], axis=3)
    v_s = jnp.concatenate([v_old, v_new], axis=3)
    kpos_s = jnp.concatenate([kpos_old, jnp.broadcast_to(pos, (b_, t_, NSA_GROUPS, t_))], axis=3)
    valid_s = jnp.concatenate([valid_old, jnp.ones((b_, t_, NSA_GROUPS, t_), dtype=bool)], axis=3)
    o_slc = gathered_attention(q, k_s, v_s, pos, kpos_s, valid_s, table_g)

    kw = rmsnorm(jnp.concatenate([win_buf[:, :, 0], kv[:, :, 4]], axis=1), p['g_k_win'])
    vw = jnp.concatenate([win_buf[:, :, 1], kv[:, :, 5]], axis=1)
    kpos_w = past - wbuf + jnp.arange(wbuf + t_, dtype=jnp.int32)
    o_win, _ = shared_key_attention(q, kw, vw, pos, kpos_w, table_g, WINDOW)

    y = merge_and_ffn(x, o_mla, nsa_combine(gates, o_cmp, o_slc, o_win), mgate, p)
    new_win = jnp.concatenate([win_buf, kv[:, :, 4:]], axis=1)[:, t_:]
    return y, mla_rows, kv[:, :, :4], new_win


def setup_inputs(seed: int = 0) -> dict:
    key = jax.random.key(seed)
    kit = iter(jax.random.split(key, 48))
    f32 = jnp.float32

    def rnd(shape, std):
        return std * jax.random.normal(next(kit), shape, f32)

    def lin(shape):
        return rnd(shape, shape[-2] ** -0.5)

    def gain(shape):
        return 1.0 + rnd(shape, 0.05)

    n_pages = PAST_LEN // PAGE_SIZE
    n_used = DEC_BATCH * n_pages
    n_pool = n_used + max(n_used // 4, 1)
    wbuf = min(WINDOW, PAST_LEN)
    x_prompt = rnd((BATCH, SEQ, D_MODEL), 1.0)
    x_sample = rnd((DEC_BATCH, DEC_SEQ, D_MODEL), 1.0)
    cache_mla = rnd((DEPTH, n_pool, PAGE_SIZE, MLA_ROW), 1.0)
    cache_nsa_kv = rnd((DEPTH, n_pool, PAGE_SIZE, 4, NSA_GROUPS, NSA_DH), 1.0)
    state_nsa_win = rnd((DEPTH, DEC_BATCH, wbuf, 2, NSA_GROUPS, NSA_DH), 1.0)
    page_table = jax.random.permutation(next(kit), n_pool)[:n_used].astype(jnp.int32).reshape(DEC_BATCH, n_pages)
    return {
        'x_prompt': x_prompt,
        'x_sample': x_sample,
        'cache_mla': cache_mla,
        'cache_nsa_kv': cache_nsa_kv,
        'state_nsa_win': state_nsa_win,
        'page_table': page_table,
        'g_attn': gain((DEPTH, D_MODEL)),
        'w_in': lin((DEPTH, D_MODEL, IN_TOTAL)),
        'g_q_lat': gain((DEPTH, Q_LORA)),
        'w_q_b': lin((DEPTH, Q_LORA, MLA_HEADS * MLA_QK)),
        'g_kv_lat': gain((DEPTH, KV_LORA)),
        'w_ukv': lin((DEPTH, KV_LORA, MLA_HEADS * (MLA_NOPE + MLA_VDIM))),
        'g_mla_q': gain((DEPTH, MLA_QK)),
        'g_mla_k': gain((DEPTH, MLA_QK)),
        'w_up_mla': lin((DEPTH, MLA_HEADS * MLA_VDIM, D_MODEL)),
        'g_nsa_q': gain((DEPTH, NSA_DH)),
        'g_k_cmp': gain((DEPTH, NSA_DH)),
        'g_k_slc': gain((DEPTH, NSA_DH)),
        'g_k_win': gain((DEPTH, NSA_DH)),
        'pe_cmp_k': rnd((DEPTH, CMP_BLOCK, NSA_DH), 0.5),
        'w_cmp_k1': lin((DEPTH, CMP_BLOCK * NSA_DH, CMP_HID)),
        'b_cmp_k1': rnd((DEPTH, CMP_HID), 0.01),
        'w_cmp_k2': lin((DEPTH, CMP_HID, NSA_DH)),
        'pe_cmp_v': rnd((DEPTH, CMP_BLOCK, NSA_DH), 0.5),
        'w_cmp_v1': lin((DEPTH, CMP_BLOCK * NSA_DH, CMP_HID)),
        'b_cmp_v1': rnd((DEPTH, CMP_HID), 0.01),
        'w_cmp_v2': lin((DEPTH, CMP_HID, NSA_DH)),
        'rel_bias': rnd((N_BUCKETS, NSA_HEADS), 0.5),
        'w_up_nsa': lin((DEPTH, NSA_HEADS * NSA_DH, D_MODEL)),
        'w_out': lin((DEPTH, D_MODEL, D_MODEL)),
        'g_ffn': gain((DEPTH, D_MODEL)),
        'w_router_group': lin((DEPTH, D_MODEL, N_GROUPS)),
        'b_router_group': rnd((DEPTH, N_GROUPS), 0.01),
        'w_router_expert': lin((DEPTH, D_MODEL, N_EXPERTS)),
        'b_router_expert': rnd((DEPTH, N_EXPERTS), 0.01),
        'w_e_gate': lin((DEPTH, N_EXPERTS, D_MODEL, D_EXPERT)),
        'w_e_up': lin((DEPTH, N_EXPERTS, D_MODEL, D_EXPERT)),
        'w_e_down': lin((DEPTH, N_EXPERTS, D_EXPERT, D_MODEL)),
    }


def reference(x_prompt, x_sample, cache_mla, cache_nsa_kv, state_nsa_win, page_table,
              g_attn, w_in, g_q_lat, w_q_b, g_kv_lat, w_ukv, g_mla_q, g_mla_k, w_up_mla,
              g_nsa_q, g_k_cmp, g_k_slc, g_k_win,
              pe_cmp_k, w_cmp_k1, b_cmp_k1, w_cmp_k2, pe_cmp_v, w_cmp_v1, b_cmp_v1, w_cmp_v2,
              rel_bias, w_up_nsa, w_out, g_ffn,
              w_router_group, b_router_group, w_router_expert, b_router_expert,
              w_e_gate, w_e_up, w_e_down):
    table_g = rel_bias.reshape(N_BUCKETS, NSA_GROUPS, NSA_HPG)
    y_p, y_s = x_prompt, x_sample
    mla_p, mla_s, nsa_p, nsa_s, win_p, win_s = [], [], [], [], [], []
    for l in range(DEPTH):
        p = {
            'g_attn': g_attn[l], 'w_in': w_in[l],
            'g_q_lat': g_q_lat[l], 'w_q_b': w_q_b[l], 'g_kv_lat': g_kv_lat[l], 'w_ukv': w_ukv[l],
            'g_mla_q': g_mla_q[l], 'g_mla_k': g_mla_k[l], 'w_up_mla': w_up_mla[l],
            'g_nsa_q': g_nsa_q[l], 'g_k_cmp': g_k_cmp[l], 'g_k_slc': g_k_slc[l], 'g_k_win': g_k_win[l],
            'pe_cmp_k': pe_cmp_k[l], 'w_cmp_k1': w_cmp_k1[l], 'b_cmp_k1': b_cmp_k1[l], 'w_cmp_k2': w_cmp_k2[l],
            'pe_cmp_v': pe_cmp_v[l], 'w_cmp_v1': w_cmp_v1[l], 'b_cmp_v1': b_cmp_v1[l], 'w_cmp_v2': w_cmp_v2[l],
            'w_up_nsa': w_up_nsa[l], 'w_out': w_out[l], 'g_ffn': g_ffn[l],
            'w_router_group': w_router_group[l], 'b_router_group': b_router_group[l],
            'w_router_expert': w_router_expert[l], 'b_router_expert': b_router_expert[l],
            'w_e_gate': w_e_gate[l], 'w_e_up': w_e_up[l], 'w_e_down': w_e_down[l],
        }
        y_p, a_rows, b_rows, c_win = prompt_layer(y_p, p, table_g)
        y_s, d_rows, e_rows, f_win = sample_layer(y_s, cache_mla, cache_nsa_kv, state_nsa_win[l], page_table, l, p, table_g)
        mla_p.append(a_rows)
        nsa_p.append(b_rows)
        win_p.append(c_win)
        mla_s.append(d_rows)
        nsa_s.append(e_rows)
        win_s.append(f_win)
    new_mla_prompt = jnp.stack(mla_p)
    new_mla_sample = jnp.stack(mla_s)
    new_nsa_prompt = jnp.stack(nsa_p)
    new_nsa_sample = jnp.stack(nsa_s)
    new_win_prompt = jnp.stack(win_p)
    new_win_sample = jnp.stack(win_s)
    return (y_p, y_s, new_mla_prompt, new_mla_sample, new_nsa_prompt, new_nsa_sample, new_win_prompt, new_win_sample)
```

```python
import functools
import math

import numpy as np
import jax
import jax.numpy as jnp
from jax import lax
from jax.experimental import pallas as pl
from jax.experimental.pallas import tpu as pltpu

F32 = jnp.float32
BF16 = jnp.bfloat16

MLA_HEADS = 8
MLA_NOPE = 64
MLA_ROPE = 32
MLA_QK = MLA_NOPE + MLA_ROPE
MLA_VDIM = 64
Q_LORA = 256
KV_LORA = 128
MLA_ROW = KV_LORA + MLA_ROPE
ROPE_BASE = 10000.0
NSA_HEADS = 8
NSA_GROUPS = 2
NSA_HPG = NSA_HEADS // NSA_GROUPS
NSA_DH = 64
CMP_BLOCK = 64
CMP_HID = 128
N_SEL = 16
WINDOW = 512
N_BUCKETS = 32
MAX_DISTANCE = 128
N_GROUPS = 4
EXPERTS_PER_GROUP = 8
N_EXPERTS = N_GROUPS * EXPERTS_PER_GROUP
D_EXPERT = 256
EPS = 1e-6
NEG = -1e30
MLA_SCALE = MLA_QK ** -0.5
NSA_SCALE = NSA_DH ** -0.5

LANES = 128
HALF = LANES // 2
SUB = 128
FAR_DIST = 113

O_CQ = 0
O_CKV = O_CQ + Q_LORA
O_PEA = O_CKV + KV_LORA
O_PEB = O_PEA + LANES
O_NQ = O_PEB + LANES
O_NKV = O_NQ + NSA_HEADS * NSA_DH
O_GATE = O_NKV + 6 * NSA_GROUPS * NSA_DH
O_MG = O_GATE + LANES

_NT_DIMS = (((1,), (1,)), ((), ()))


def _nt(a, b):
    return lax.dot_general(a, b, _NT_DIMS, preferred_element_type=F32)


def _nn(a, b):
    return jnp.dot(a, b, preferred_element_type=F32)


def _params(sem, vmem_mb):
    return pltpu.CompilerParams(dimension_semantics=sem, vmem_limit_bytes=vmem_mb * 1024 * 1024)


def _lane_iota(shape):
    return lax.broadcasted_iota(jnp.int32, shape, len(shape) - 1)


def _pair_rms(x, gpair):
    lo = _lane_iota(x.shape) < HALF
    sq = x * x
    s_lo = jnp.sum(jnp.where(lo, sq, 0.0), axis=-1, keepdims=True)
    s_hi = jnp.sum(jnp.where(lo, 0.0, sq), axis=-1, keepdims=True)
    rinv = jnp.where(lo, lax.rsqrt(s_lo / NSA_DH + EPS), lax.rsqrt(s_hi / NSA_DH + EPS))
    return x * rinv * gpair


def _online_update(s, v, m_ref, l_ref, acc_ref, idx, v_transposed=False):
    m_prev = m_ref[idx]
    m_new = jnp.maximum(m_prev, jnp.max(s, axis=1, keepdims=True))
    alpha = jnp.exp(m_prev - m_new)
    p = jnp.exp(s - m_new[:, :1])
    l_ref[idx] = alpha * l_ref[idx] + jnp.sum(p, axis=1, keepdims=True)
    m_ref[idx] = m_new
    pb = p.astype(BF16)
    pv = _nt(pb, v) if v_transposed else _nn(pb, v)
    acc_ref[idx] = acc_ref[idx] * alpha + pv


def _topk_select(scores, n_top):
    lane = _lane_iota(scores.shape).astype(F32)
    big = float(scores.shape[-1])

    def body(_, carry):
        sc, sel = carry
        m = jnp.max(sc, axis=1, keepdims=True)
        first = jnp.min(jnp.where(sc == m, lane, big), axis=1, keepdims=True)
        hit = lane == first
        sel = jnp.where(hit & (m >= 0.0), 1.0, sel)
        sc = jnp.where(hit, -2.0, sc)
        return sc, sel

    _, sel = lax.fori_loop(0, n_top, body, (scores, jnp.zeros_like(scores)))
    return sel


def _bias_lookup_kernel(tbl_ref, bucket_ref, neg_ref, out_ref):
    tbl = tbl_ref[...]
    b = bucket_ref[...]
    acc = jnp.zeros(out_ref.shape, F32)
    for k in range(N_BUCKETS):
        acc = jnp.where(b == k, tbl[:, k:k + 1], acc)
    out_ref[...] = acc - tbl[:, N_BUCKETS - 1:N_BUCKETS] + neg_ref[...]


def _bias_lookup(table_t, bucket, neg):
    n = bucket.shape[1]
    chunk = 8192
    assert n % chunk == 0
    return pl.pallas_call(
        _bias_lookup_kernel,
        out_shape=jax.ShapeDtypeStruct((NSA_HEADS, n), F32),
        grid=(n // chunk,),
        in_specs=[pl.BlockSpec((NSA_HEADS, N_BUCKETS), lambda i: (0, 0)),
                  pl.BlockSpec((1, chunk), lambda i: (0, i)),
                  pl.BlockSpec((1, chunk), lambda i: (0, i))],
        out_specs=pl.BlockSpec((NSA_HEADS, chunk), lambda i: (0, i)),
        compiler_params=_params(("arbitrary",), 32),
    )(table_t, bucket, neg)


def _t5_bucket_np(dist):
    n = np.maximum(dist, 0)
    max_exact = N_BUCKETS // 2
    log_ratio = np.log(np.maximum(n, 1).astype(np.float32) / np.float32(max_exact)) / np.float32(
        math.log(MAX_DISTANCE / max_exact))
    large = max_exact + (log_ratio.astype(np.float32) * (N_BUCKETS - max_exact)).astype(np.int32)
    return np.where(n < max_exact, n, np.minimum(large, N_BUCKETS - 1)).astype(np.int32)


def _bias_tables(rel_bias, tq_win, past, dec_seq, wbuf, nbs):
    pieces = {}

    def add(name, dist, valid):
        dist = np.asarray(dist)
        pieces[name] = (dist.shape, _t5_bucket_np(dist).reshape(-1),
                        np.where(np.asarray(valid), 0.0, NEG).astype(np.float32).reshape(-1))

    d = np.arange(2 * SUB)
    add("f", d, np.ones_like(d, bool))
    i = np.arange(SUB)[:, None]
    j = np.arange(SUB)[None, :]
    add("td", i - j, i >= j)
    add("ts", SUB + i - j, np.ones((SUB, SUB), bool))
    nkb = WINDOW // tq_win + 1
    iw = np.arange(tq_win)[:, None]
    cw = np.arange(nkb * tq_win)[None, :]
    dw = WINDOW + iw - cw
    add("win", dw, (dw >= 0) & (dw < WINDOW))
    tok = np.arange(dec_seq)[:, None]
    pos = past + tok
    jc = np.arange(nbs)[None, :]
    dc = pos - (jc * CMP_BLOCK + CMP_BLOCK - 1)
    add("s_cmp", dc, dc >= 0)
    wpad = wbuf + LANES
    w = np.arange(wpad)[None, :]
    dsw = wbuf + tok - w
    add("s_win", dsw, (dsw >= 0) & (dsw < WINDOW) & (w < wbuf + dec_seq))
    n = np.arange(LANES)[None, :]
    add("s_new", tok - n, (n <= tok) & (n < dec_seq))
    kl = np.arange(2 * LANES)[None, :]
    add("s_last", pos - (past - 2 * LANES + kl), np.ones((dec_seq, 2 * LANES), bool))

    total = sum(v[1].size for v in pieces.values())
    padded = -(-total // 8192) * 8192
    bucket = np.zeros((1, padded), np.int32)
    neg = np.zeros((1, padded), np.float32)
    off = 0
    spans = {}
    for name, (shape, b, ng) in pieces.items():
        bucket[0, off:off + b.size] = b
        neg[0, off:off + b.size] = ng
        spans[name] = (off, shape)
        off += b.size
    flat = _bias_lookup(rel_bias.T.astype(F32), jnp.asarray(bucket), jnp.asarray(neg))
    out = {}
    for name, (o, shape) in spans.items():
        out[name] = flat[:, o:o + int(np.prod(shape))].reshape((NSA_HEADS,) + tuple(shape))
    return out


def _sample_rows(tile):
    h, t, w = tile.shape
    x = tile.reshape(NSA_GROUPS, NSA_HPG, t, w)
    return jnp.transpose(x, (1, 0, 2, 3)).reshape(h * t, w)


def _proj_kernel(x_ref, cs_ref, gattn_ref, win_ref, gq_ref, wq_ref, gkv_ref, wk_ref, wv_ref,
                 gmq_ref, gmk_ref, gnq_ref, gslc_ref, gwin_ref,
                 rows_ref, qm_ref, km_ref, vm_ref, nq_ref, nsa4_ref, win2_ref,
                 slck_ref, slcv_ref, wink_ref, winv_ref, gates_ref, mg_ref, *extra, d_model):
    x = x_ref[...]
    ms = jnp.mean(x * x, axis=-1, keepdims=True)
    h = (x * lax.rsqrt(ms + EPS) * gattn_ref[...]).astype(BF16)

    def seg(o, n):
        return _nn(h, win_ref[:, o:o + n])

    cs = cs_ref[...]
    cos_t = cs[:, :LANES]
    sin_t = cs[:, LANES:]

    ckv = seg(O_CKV, KV_LORA)
    lat = ckv * lax.rsqrt(jnp.mean(ckv * ckv, axis=-1, keepdims=True) + EPS) * gkv_ref[...]
    kpe = seg(O_PEA, LANES) * cos_t + seg(O_PEB, LANES) * sin_t
    rows_ref[:, 0:KV_LORA] = lat
    rows_ref[:, KV_LORA:MLA_ROW] = pltpu.roll(kpe, HALF, 1)[:, 0:MLA_ROPE]

    cq = seg(O_CQ, Q_LORA)
    cqn = (cq * lax.rsqrt(jnp.mean(cq * cq, axis=-1, keepdims=True) + EPS) * gq_ref[...]).astype(BF16)
    qab = _nn(cqn, wq_ref[...])
    lat_b = lat.astype(BF16)
    kall = _nn(lat_b, wk_ref[...])
    nh = MLA_HEADS * LANES
    for hd in range(MLA_HEADS):
        sl = slice(LANES * hd, LANES * (hd + 1))
        qh = qab[:, sl] * cos_t + qab[:, nh + LANES * hd:nh + LANES * (hd + 1)] * sin_t
        qn = qh * lax.rsqrt(jnp.sum(qh * qh, axis=-1, keepdims=True) / MLA_QK + EPS) * gmq_ref[...] * MLA_SCALE
        qm_ref[:, sl] = qn.astype(BF16)
        if extra:
            extra[0][:, sl] = qn
        kh = kall[:, sl] + kpe
        kn = kh * lax.rsqrt(jnp.sum(kh * kh, axis=-1, keepdims=True) / MLA_QK + EPS) * gmk_ref[...]
        km_ref[:, sl] = kn.astype(BF16)
    vm_ref[...] = _nn(lat_b, wv_ref[...]).astype(BF16)

    nq = seg(O_NQ, NSA_HEADS * NSA_DH)
    for j in range(NSA_HPG):
        sl = slice(LANES * j, LANES * (j + 1))
        qj = _pair_rms(nq[:, sl], gnq_ref[...]) * NSA_SCALE
        nq_ref[:, sl] = qj.astype(BF16)
        if extra:
            extra[1][:, sl] = (qj * gslc_ref[...]).astype(BF16)
            extra[2][:, sl] = (qj * gwin_ref[...]).astype(BF16)
    nkv = seg(O_NKV, 6 * LANES)
    nsa4_ref[...] = nkv[:, :4 * LANES]
    win2_ref[...] = nkv[:, 4 * LANES:]
    slck_ref[...] = _pair_rms(nkv[:, 2 * LANES:3 * LANES], gslc_ref[...]).astype(BF16)
    slcv_ref[...] = nkv[:, 3 * LANES:4 * LANES].astype(BF16)
    wink_ref[...] = _pair_rms(nkv[:, 4 * LANES:5 * LANES], gwin_ref[...]).astype(BF16)
    winv_ref[...] = nkv[:, 5 * LANES:].astype(BF16)
    gates_ref[...] = jax.nn.sigmoid(seg(O_GATE, LANES))
    mg_ref[...] = jax.nn.sigmoid(seg(O_MG, 2 * d_model))


def _project(x2d, cs, w, tm, sample):
    t, d = x2d.shape
    tm = min(tm, t)
    assert t % tm == 0 and cs.shape[0] % tm == 0
    n_cs = cs.shape[0] // tm
    nin = w["w_in"].shape[1]
    full = lambda a: pl.BlockSpec(a.shape, lambda i: (0,) * a.ndim)
    tok = lambda n: pl.BlockSpec((tm, n), lambda i: (i, 0))
    weights = [w["g_attn"], w["w_in"], w["g_q_lat"], w["w_q"], w["g_kv_lat"], w["w_k"], w["w_v"],
               w["g_mla_q"], w["g_mla_k"], w["g_nsa_q"], w["g_k_slc"], w["g_k_win"]]
    outs = [("rows", MLA_ROW, F32), ("qm", 8 * LANES, BF16), ("km", 8 * LANES, BF16), ("vm", 4 * LANES, BF16),
            ("nq", 4 * LANES, BF16), ("nsa4", 4 * LANES, F32), ("win2", 2 * LANES, F32),
            ("slck", LANES, BF16), ("slcv", LANES, BF16), ("wink", LANES, BF16), ("winv", LANES, BF16),
            ("gates", LANES, F32), ("mg", 2 * d, F32)]
    if sample:
        outs += [("qmf", 8 * LANES, F32), ("nq_slc", 4 * LANES, BF16), ("nq_win", 4 * LANES, BF16)]
    res = pl.pallas_call(
        functools.partial(_proj_kernel, d_model=d),
        out_shape=[jax.ShapeDtypeStruct((t, n), dt) for _, n, dt in outs],
        grid=(t // tm,),
        in_specs=[tok(d), pl.BlockSpec((tm, 2 * LANES), lambda i: (i % n_cs, 0))] + [full(a) for a in weights],
        out_specs=[tok(n) for _, n, _ in outs],
        compiler_params=_params(("arbitrary",), 56),
    )(x2d, cs, *weights)
    return {name: r for (name, _, _), r in zip(outs, res)}


def _mla_prefill_kernel(q_ref, k_ref, v_ref, o_ref, m_sc, l_sc, acc_sc, *, tq):
    qi = pl.program_id(2)
    ki = pl.program_id(3)

    @pl.when(ki == 0)
    def _():
        m_sc[...] = jnp.full(m_sc.shape, NEG, F32)
        l_sc[...] = jnp.zeros(l_sc.shape, F32)
        acc_sc[...] = jnp.zeros(acc_sc.shape, F32)

    def body(diag):
        v = v_ref[0]
        for hh in range(2):
            sl = slice(LANES * hh, LANES * (hh + 1))
            s = _nt(q_ref[0, :, sl], k_ref[0, :, sl])
            if diag:
                row = lax.broadcasted_iota(jnp.int32, s.shape, 0)
                col = lax.broadcasted_iota(jnp.int32, s.shape, 1)
                s = jnp.where(col <= row, s, NEG)
            _online_update(s, v, m_sc, l_sc, acc_sc, hh)

    @pl.when(ki < qi)
    def _():
        body(False)

    @pl.when(ki == qi)
    def _():
        body(True)
        lo = _lane_iota((tq, LANES)) < HALF
        o_ref[0] = jnp.where(lo, acc_sc[0] / l_sc[0], acc_sc[1] / l_sc[1]).astype(BF16)


def _mla_prefill(qm, km, vm, tq):
    b, s, _ = qm.shape
    tq = min(tq, s)
    nq = s // tq
    kmap = lambda bb, j, qi, ki: (bb, jnp.minimum(ki, qi), j)
    return pl.pallas_call(
        functools.partial(_mla_prefill_kernel, tq=tq),
        out_shape=jax.ShapeDtypeStruct((b, s, MLA_HEADS * MLA_VDIM), BF16),
        grid=(b, MLA_HEADS // 2, nq, nq),
        in_specs=[pl.BlockSpec((1, tq, 2 * LANES), lambda bb, j, qi, ki: (bb, qi, j)),
                  pl.BlockSpec((1, tq, 2 * LANES), kmap),
                  pl.BlockSpec((1, tq, LANES), kmap)],
        out_specs=pl.BlockSpec((1, tq, LANES), lambda bb, j, qi, ki: (bb, qi, j)),
        scratch_shapes=[pltpu.VMEM((2, tq, LANES), F32)] * 3,
        compiler_params=_params(("arbitrary",) * 4, 48),
    )(qm, km, vm)


def _compress_kernel(pt_ref, cache_ref, wt_ref, pe_ref, b1_ref, w2k_ref, w2v_ref, gk_ref,
                     kc_ref, vc_ref, bufk, bufv, sem, acc, *, pages, nchunk, lane0):
    s = pl.program_id(0)
    c = pl.program_id(1)
    step = s * nchunk + c
    nsteps = pl.num_programs(0) * nchunk
    slot = step % 2
    page_rows = 2 * CMP_BLOCK

    def copies(seq, chunk, slot_, start):
        def one(p, carry):
            page = pt_ref[seq, chunk * pages + p]
            row0 = pl.multiple_of((slot_ * pages + p) * page_rows, page_rows)
            for kv, dst in enumerate((bufk, bufv)):
                cp = pltpu.make_async_copy(cache_ref.at[page, :, pl.ds(lane0 + LANES * kv, LANES)],
                                           dst.at[pl.ds(row0, page_rows), :], sem.at[slot_, kv])
                if start:
                    cp.start()
                else:
                    cp.wait()
            return carry
        lax.fori_loop(0, pages, one, 0)

    @pl.when(step == 0)
    def _():
        copies(s, c, slot, True)

    @pl.when(step + 1 < nsteps)
    def _():
        nxt = step + 1
        copies(nxt // nchunk, nxt % nchunk, 1 - slot, True)

    copies(s, c, slot, False)
    acc[...] = jnp.zeros(acc.shape, F32)
    base = slot * pages * page_rows

    def rbody(r, carry):
        def rows_of(src):
            return jnp.concatenate([src[pl.ds(base + r, pages, stride=page_rows), :],
                                    src[pl.ds(base + CMP_BLOCK + r, pages, stride=page_rows), :]], axis=0)
        xr = (jnp.concatenate([rows_of(bufk), rows_of(bufv)], axis=1) + pe_ref[r]).astype(BF16)
        acc[...] += _nt(wt_ref[r], xr)
        return carry
    lax.fori_loop(0, CMP_BLOCK, rbody, 0)

    hid = jax.nn.gelu(acc[...] + b1_ref[...])
    kc = _nn(w2k_ref[...], hid[:2 * CMP_HID].astype(BF16))
    vc = _nn(w2v_ref[...], hid[2 * CMP_HID:].astype(BF16))
    gk = gk_ref[...]
    parts = []
    for g in range(NSA_GROUPS):
        part = kc[NSA_DH * g:NSA_DH * (g + 1)]
        ss = jnp.sum(part * part, axis=0, keepdims=True)
        parts.append(part * lax.rsqrt(ss / NSA_DH + EPS) * gk[NSA_DH * g:NSA_DH * (g + 1)])
    kc_ref[0] = jnp.concatenate(parts, axis=0).astype(BF16)
    vc_ref[0] = vc.astype(BF16)


def _compress(page_table, cache3d, w, pages_per_step):
    nseq, npages = page_table.shape
    pages = min(pages_per_step, npages)
    assert npages % pages == 0
    nchunk = npages // pages
    nblk = 2 * pages
    full = lambda a: pl.BlockSpec(a.shape, lambda s, c, pt: (0,) * a.ndim)
    consts = [w["cmp_wt"], w["cmp_pe"], w["cmp_b1"], w["cmp_w2k"], w["cmp_w2v"], w["g_k_cmp_col"]]
    out_spec = pl.BlockSpec((1, LANES, nblk), lambda s, c, pt: (s, 0, c))
    kc, vc = pl.pallas_call(
        functools.partial(_compress_kernel, pages=pages, nchunk=nchunk, lane0=0),
        out_shape=[jax.ShapeDtypeStruct((nseq, LANES, nchunk * nblk), BF16)] * 2,
        grid_spec=pltpu.PrefetchScalarGridSpec(
            num_scalar_prefetch=1,
            grid=(nseq, nchunk),
            in_specs=[pl.BlockSpec(memory_space=pl.ANY)] + [full(a) for a in consts],
            out_specs=[out_spec, out_spec],
            scratch_shapes=[pltpu.VMEM((2 * pages * 2 * CMP_BLOCK, LANES), F32),
                            pltpu.VMEM((2 * pages * 2 * CMP_BLOCK, LANES), F32),
                            pltpu.SemaphoreType.DMA((2, 2)),
                            pltpu.VMEM((4 * CMP_HID, nblk), F32)]),
        compiler_params=_params(("arbitrary", "arbitrary"), 56),
    )(page_table, cache3d, *consts)

    def unpermute(a):
        a = a.reshape(nseq, LANES, nchunk, 2, pages)
        return jnp.transpose(a, (0, 1, 2, 4, 3)).reshape(nseq, LANES, nchunk * nblk)
    return unpermute(kc), unpermute(vc)


def _cmp_prefill_kernel(q_ref, kc_ref, vc_ref, cols_ref, gates_ref, o_ref, selb_ref, *, tq, nb, nbp):
    qi = pl.program_id(1)
    kc = kc_ref[0]
    vc = vc_ref[0]
    cols = cols_ref[...]
    gates = gates_ref[0]
    t = lax.broadcasted_iota(jnp.int32, (tq, nb), 0) + qi * tq
    j = lax.broadcasted_iota(jnp.int32, (tq, nb), 1)
    cb = t // CMP_BLOCK
    r = t % CMP_BLOCK
    u = cb - j
    vis = (u >= 1) | ((u == 0) & (r == CMP_BLOCK - 1))
    visf = jnp.where(vis, 1.0, 0.0)
    lo = _lane_iota((tq, LANES)) < HALF
    imp = [jnp.zeros((tq, nb), F32), jnp.zeros((tq, nb), F32)]
    outs = []
    for jp in range(NSA_HPG):
        qv = q_ref[0, :, LANES * jp:LANES * (jp + 1)]
        halves = []
        for half in range(NSA_GROUPS):
            hd = jp + NSA_HPG * half
            qh = jnp.where(lo if half == 0 else jnp.logical_not(lo), qv, jnp.zeros_like(qv))
            s = _nn(qh, kc)
            bias = jnp.where(u == 1, cols[:, hd:hd + 1],
                             jnp.where(u == 2, cols[:, 8 + hd:9 + hd],
                                       jnp.where(u == 0, cols[:, 16 + hd:17 + hd], 0.0)))
            s = jnp.where(vis, s + bias, NEG)
            m = jnp.max(s, axis=1, keepdims=True)
            p = jnp.exp(s - m) * visf
            p = p / jnp.maximum(jnp.sum(p, axis=1, keepdims=True), 1e-30)
            imp[half] = imp[half] + p
            halves.append(_nt(p.astype(BF16), vc) * gates[:, hd:hd + 1])
        outs.append(jnp.where(lo, halves[0], halves[1]))
    o_ref[0] = jnp.concatenate(outs, axis=1)
    cand = j < cb
    sels = []
    for g in range(NSA_GROUPS):
        sel = _topk_select(jnp.where(cand, imp[g], -1.0), N_SEL - 1)
        sb = jnp.where((sel > 0.0) | (j == cb), 0.0, NEG)
        if nbp > nb:
            sb = jnp.concatenate([sb, jnp.full((tq, nbp - nb), NEG, F32)], axis=1)
        sels.append(sb)
    selb_ref[0] = jnp.concatenate(sels, axis=1).astype(BF16)


def _cmp_prefill(nq, kct, vct, cols, gates, tq):
    b, s, _ = nq.shape
    tq = min(tq, s)
    nb = kct.shape[2]
    nbp = -(-nb // LANES) * LANES
    return pl.pallas_call(
        functools.partial(_cmp_prefill_kernel, tq=tq, nb=nb, nbp=nbp),
        out_shape=[jax.ShapeDtypeStruct((b, s, 4 * LANES), F32),
                   jax.ShapeDtypeStruct((b, s, NSA_GROUPS * nbp), BF16)],
        grid=(b, s // tq),
        in_specs=[pl.BlockSpec((1, tq, 4 * LANES), lambda bb, qi: (bb, qi, 0)),
                  pl.BlockSpec((1, LANES, nb), lambda bb, qi: (bb, 0, 0)),
                  pl.BlockSpec((1, LANES, nb), lambda bb, qi: (bb, 0, 0)),
                  pl.BlockSpec((tq, LANES), lambda bb, qi: (0, 0)),
                  pl.BlockSpec((1, tq, LANES), lambda bb, qi: (bb, qi, 0))],
        out_specs=[pl.BlockSpec((1, tq, 4 * LANES), lambda bb, qi: (bb, qi, 0)),
                   pl.BlockSpec((1, tq, NSA_GROUPS * nbp), lambda bb, qi: (bb, qi, 0))],
        compiler_params=_params(("arbitrary", "arbitrary"), 48),
    )(nq, kct, vct, cols, gates)


def _slc_prefill_kernel(q_ref, selb_ref, k_ref, v_ref, td_ref, ts_ref, gates_ref, o_ref,
                        m_sc, l_sc, acc_sc, *, tq, nbp):
    qi = pl.program_id(1)
    ki = pl.program_id(2)
    nsub = tq // SUB

    @pl.when(ki == 0)
    def _():
        m_sc[...] = jnp.full(m_sc.shape, NEG, F32)
        l_sc[...] = jnp.zeros(l_sc.shape, F32)
        acc_sc[...] = jnp.zeros(acc_sc.shape, F32)

    def add_bias(s, hd, kind):
        if kind == 0:
            return s
        td = td_ref[hd]
        ts = ts_ref[hd]
        rows = []
        for a in range(nsub):
            row = s[SUB * a:SUB * (a + 1), :]
            pcs = [row[:, SUB * c:SUB * (c + 1)] for c in range(nsub)]
            if kind == 2:
                pcs[a] = pcs[a] + td
                if a >= 1:
                    pcs[a - 1] = pcs[a - 1] + ts
            elif a == 0:
                pcs[nsub - 1] = pcs[nsub - 1] + ts
            rows.append(jnp.concatenate(pcs, axis=1) if nsub > 1 else pcs[0])
        return jnp.concatenate(rows, axis=0) if nsub > 1 else rows[0]

    def body(kind):
        k = k_ref[0]
        v = v_ref[0]
        kpos = lax.broadcasted_iota(jnp.int32, (tq, nbp), 0) + ki * tq
        blk = lax.broadcasted_iota(jnp.int32, (tq, nbp), 1)
        onehot = jnp.where(kpos // CMP_BLOCK == blk, 1.0, 0.0).astype(BF16)
        kaug = jnp.concatenate([k, onehot], axis=1)
        lo = _lane_iota((tq, LANES)) < HALF
        for jp in range(NSA_HPG):
            qv = q_ref[0, :, LANES * jp:LANES * (jp + 1)]
            for half in range(NSA_GROUPS):
                hd = jp + NSA_HPG * half
                qh = jnp.where(lo if half == 0 else jnp.logical_not(lo), qv, jnp.zeros_like(qv))
                qaug = jnp.concatenate([qh, selb_ref[0, :, nbp * half:nbp * (half + 1)]], axis=1)
                s = add_bias(_nt(qaug, kaug), hd, kind)
                _online_update(s, v, m_sc, l_sc, acc_sc, hd)

    @pl.when(ki < qi - 1)
    def _():
        body(0)

    @pl.when(ki == qi - 1)
    def _():
        body(1)

    @pl.when(ki == qi)
    def _():
        body(2)
        gates = gates_ref[0]
        lo = _lane_iota((tq, LANES)) < HALF
        outs = []
        for jp in range(NSA_HPG):
            h0, h1 = jp, jp + NSA_HPG
            o0 = acc_sc[h0] / l_sc[h0] * gates[:, 8 + h0:9 + h0]
            o1 = acc_sc[h1] / l_sc[h1] * gates[:, 8 + h1:9 + h1]
            outs.append(jnp.where(lo, o0, o1))
        o_ref[0] = jnp.concatenate(outs, axis=1)


def _slc_prefill(nq, selb, slck, slcv, td, ts, gates, tq):
    b, s, _ = nq.shape
    tq = min(tq, s)
    nbp = selb.shape[2] // NSA_GROUPS
    nq_t = s // tq
    kmap = lambda bb, qi, ki: (bb, jnp.minimum(ki, qi), 0)
    qmap = lambda bb, qi, ki: (bb, qi, 0)
    return pl.pallas_call(
        functools.partial(_slc_prefill_kernel, tq=tq, nbp=nbp),
        out_shape=jax.ShapeDtypeStruct((b, s, 4 * LANES), F32),
        grid=(b, nq_t, nq_t),
        in_specs=[pl.BlockSpec((1, tq, 4 * LANES), qmap),
                  pl.BlockSpec((1, tq, NSA_GROUPS * nbp), qmap),
                  pl.BlockSpec((1, tq, LANES), kmap),
                  pl.BlockSpec((1, tq, LANES), kmap),
                  pl.BlockSpec((NSA_HEADS, SUB, SUB), lambda bb, qi, ki: (0, 0, 0)),
                  pl.BlockSpec((NSA_HEADS, SUB, SUB), lambda bb, qi, ki: (0, 0, 0)),
                  pl.BlockSpec((1, tq, LANES), qmap)],
        out_specs=pl.BlockSpec((1, tq, 4 * LANES), qmap),
        scratch_shapes=[pltpu.VMEM((NSA_HEADS, tq, LANES), F32)] * 3,
        compiler_params=_params(("arbitrary",) * 3, 48),
    )(nq, selb, slck, slcv, td, ts, gates)


def _win_prefill_kernel(q_ref, *refs, tq, nkb):
    k_refs = refs[:nkb]
    v_refs = refs[nkb:2 * nkb]
    bias_ref, gates_ref, o_ref = refs[2 * nkb:]
    qi = pl.program_id(1)
    kcat = jnp.concatenate([kr[0] for kr in k_refs], axis=0)
    vcat = jnp.concatenate([vr[0] for vr in v_refs], axis=0)
    gates = gates_ref[0]
    col_blk = lax.broadcasted_iota(jnp.int32, (tq, nkb * tq), 1) // tq
    in_seq = col_blk + qi >= nkb - 1
    lo = _lane_iota((tq, LANES)) < HALF
    outs = []
    for jp in range(NSA_HPG):
        qv = q_ref[0, :, LANES * jp:LANES * (jp + 1)]
        halves = []
        for half in range(NSA_GROUPS):
            hd = jp + NSA_HPG * half
            qh = jnp.where(lo if half == 0 else jnp.logical_not(lo), qv, jnp.zeros_like(qv))
            s = jnp.where(in_seq, _nt(qh, kcat) + bias_ref[hd], NEG)
            m = jnp.max(s, axis=1, keepdims=True)
            p = jnp.exp(s - m)
            inv = gates[:, 16 + hd:17 + hd] / jnp.sum(p, axis=1, keepdims=True)
            halves.append(_nn(p.astype(BF16), vcat) * inv)
        outs.append(jnp.where(lo, halves[0], halves[1]))
    o_ref[0] = jnp.concatenate(outs, axis=1)


def _win_prefill(nq, wink, winv, bias, gates, tq):
    b, s, _ = nq.shape
    nkb = bias.shape[2] // bias.shape[1]
    assert bias.shape[1] == tq and s % tq == 0
    qmap = lambda bb, qi: (bb, qi, 0)
    kmaps = [(lambda bb, qi, o=o: (bb, jnp.maximum(qi - o, 0), 0)) for o in range(nkb - 1, -1, -1)]
    kspecs = [pl.BlockSpec((1, tq, LANES), km) for km in kmaps]
    return pl.pallas_call(
        functools.partial(_win_prefill_kernel, tq=tq, nkb=nkb),
        out_shape=jax.ShapeDtypeStruct((b, s, 4 * LANES), F32),
        grid=(b, s // tq),
        in_specs=[pl.BlockSpec((1, tq, 4 * LANES), qmap)] + kspecs + kspecs +
                 [pl.BlockSpec(bias.shape, lambda bb, qi: (0, 0, 0)), pl.BlockSpec((1, tq, LANES), qmap)],
        out_specs=pl.BlockSpec((1, tq, 4 * LANES), qmap),
        compiler_params=_params(("arbitrary", "arbitrary"), 56),
    )(nq, *([wink] * nkb), *([winv] * nkb), bias, gates)


def _merge_kernel(x_ref, omla_ref, oc_ref, os_ref, ow_ref, mg_ref, wum_ref, wun_ref, wout_ref,
                  gffn_ref, wr_ref, br_ref, x1_ref, h2_ref, comb_ref, *, d_model):
    o_nsa = (oc_ref[...] + os_ref[...]) + ow_ref[...]
    ym = _nn(omla_ref[...], wum_ref[...])
    yn = _nn(o_nsa.astype(BF16), wun_ref[...])
    mg = mg_ref[...]
    y = mg[:, :d_model] * ym + mg[:, d_model:] * yn
    x1 = x_ref[...] + _nn(y.astype(BF16), wout_ref[...])
    x1_ref[...] = x1
    h2 = x1 * lax.rsqrt(jnp.mean(x1 * x1, axis=-1, keepdims=True) + EPS) * gffn_ref[...]
    h2b = h2.astype(BF16)
    h2_ref[...] = h2b
    logits = _nn(h2b, wr_ref[...]) + br_ref[...]
    lane = _lane_iota(logits.shape).astype(F32)
    is_g = (lane >= N_EXPERTS) & (lane < N_EXPERTS + N_GROUPS)
    gl = jnp.where(is_g, logits, NEG)
    gmax = jnp.max(gl, axis=1, keepdims=True)
    gsel = jnp.min(jnp.where(gl == gmax, lane, 1e9), axis=1, keepdims=True) - N_EXPERTS
    gw = 1.0 / jnp.sum(jnp.exp(gl - gmax), axis=1, keepdims=True)
    ing = (lane >= gsel * EXPERTS_PER_GROUP) & (lane < (gsel + 1.0) * EXPERTS_PER_GROUP)
    el = jnp.where(ing, logits, NEG)
    ex = jnp.exp(el - jnp.max(el, axis=1, keepdims=True))
    p = ex / jnp.sum(ex, axis=1, keepdims=True)
    pin = jnp.where(ing, p, -1.0)
    p1 = jnp.max(pin, axis=1, keepdims=True)
    i1 = jnp.min(jnp.where(pin == p1, lane, 1e9), axis=1, keepdims=True)
    prest = jnp.where(lane == i1, -1.0, pin)
    p2 = jnp.max(prest, axis=1, keepdims=True)
    i2 = jnp.min(jnp.where(prest == p2, lane, 1e9), axis=1, keepdims=True)
    tot = p1 + p2
    comb_ref[...] = jnp.where(lane == i1, gw * p1 / tot, jnp.where(lane == i2, gw * p2 / tot, 0.0))


def _merge(x2d, omla, oc, osl, ow, mg, w, tm):
    t, d = x2d.shape
    tm = min(tm, t)
    full = lambda a: pl.BlockSpec(a.shape, lambda i: (0,) * a.ndim)
    tok = lambda n: pl.BlockSpec((tm, n), lambda i: (i, 0))
    weights = [w["w_up_mla"], w["w_up_nsa"], w["w_out"], w["g_ffn"], w["w_router"], w["b_router"]]
    return pl.pallas_call(
        functools.partial(_merge_kernel, d_model=d),
        out_shape=[jax.ShapeDtypeStruct((t, d), F32), jax.ShapeDtypeStruct((t, d), BF16),
                   jax.ShapeDtypeStruct((t, LANES), F32)],
        grid=(t // tm,),
        in_specs=[tok(d), tok(4 * LANES), tok(4 * LANES), tok(4 * LANES), tok(4 * LANES), tok(2 * d)] +
                 [full(a) for a in weights],
        out_specs=[tok(d), tok(d), tok(LANES)],
        compiler_params=_params(("arbitrary",), 48),
    )(x2d, omla, oc, osl, ow, mg, *weights)


def _moe_kernel(h2_ref, x1_ref, comb_ref, wgu_ref, wd_ref, o_ref):
    e = pl.program_id(1)

    @pl.when(e == 0)
    def _():
        o_ref[...] = x1_ref[...]

    gu = _nn(h2_ref[...], wgu_ref[0])
    hid = jax.nn.silu(gu[:, :D_EXPERT]) * gu[:, D_EXPERT:]
    comb = comb_ref[...]
    ce = jnp.sum(jnp.where(_lane_iota(comb.shape) == e, comb, 0.0), axis=1, keepdims=True)
    o_ref[...] += _nn((hid * ce).astype(BF16), wd_ref[0])


def _moe(h2, x1, comb, w, tm):
    t, d = x1.shape
    tm = min(tm, t)
    tok = lambda n: pl.BlockSpec((tm, n), lambda i, e: (i, 0))
    return pl.pallas_call(
        _moe_kernel,
        out_shape=jax.ShapeDtypeStruct((t, d), F32),
        grid=(t // tm, N_EXPERTS),
        in_specs=[tok(d), tok(d), tok(LANES),
                  pl.BlockSpec((1, d, 2 * D_EXPERT), lambda i, e: (e, 0, 0)),
                  pl.BlockSpec((1, D_EXPERT, d), lambda i, e: (e, 0, 0))],
        out_specs=tok(d),
        compiler_params=_params(("arbitrary", "arbitrary"), 48),
    )(h2, x1, comb, w["w_gu"], w["w_down"])


def _mla_qabs_kernel(q_ref, g_ref, w_ref, o_ref):
    g = g_ref[...]
    for hd in range(MLA_HEADS):
        qg = (q_ref[:, LANES * hd:LANES * (hd + 1)] * g).astype(BF16)
        o_ref[:, 2 * LANES * hd:2 * LANES * (hd + 1)] = _nn(qg, w_ref[hd]).astype(BF16)


def _mla_qabs(qmf, w):
    t = qmf.shape[0]
    return pl.pallas_call(
        _mla_qabs_kernel,
        out_shape=jax.ShapeDtypeStruct((t, MLA_HEADS * 2 * LANES), BF16),
        grid=(1,),
        in_specs=[pl.BlockSpec(qmf.shape, lambda i: (0, 0)),
                  pl.BlockSpec((1, LANES), lambda i: (0, 0)),
                  pl.BlockSpec(w["w_abs"].shape, lambda i: (0, 0, 0))],
        out_specs=pl.BlockSpec((t, MLA_HEADS * 2 * LANES), lambda i: (0, 0)),
        compiler_params=_params(("arbitrary",), 32),
    )(qmf, w["g_mla_k"], w["w_abs"])


def _mla_decode_kernel(pt_ref, cache_ref, qabs_ref, wkt_ref, newrows_ref, newmask_ref, o_ref,
                       buf, sem, m_sc, l_sc, acc_sc, *, pages, nchunk, page_rows, sub, dec_seq):
    s = pl.program_id(0)
    c = pl.program_id(1)
    step = s * nchunk + c
    nsteps = pl.num_programs(0) * nchunk
    slot = step % 2
    nk = wkt_ref.shape[0]

    def copies(seq, chunk, slot_, start):
        def one(p, carry):
            page = pt_ref[seq, chunk * pages + p]
            row0 = pl.multiple_of((slot_ * pages + p) * page_rows, page_rows)
            cp = pltpu.make_async_copy(cache_ref.at[page], buf.at[pl.ds(row0, page_rows), :], sem.at[slot_])
            if start:
                cp.start()
            else:
                cp.wait()
            return carry
        lax.fori_loop(0, pages, one, 0)

    @pl.when(step == 0)
    def _():
        copies(s, c, slot, True)

    @pl.when(step + 1 < nsteps)
    def _():
        nxt = step + 1
        copies(nxt // nchunk, nxt % nchunk, 1 - slot, True)

    @pl.when(c == 0)
    def _():
        m_sc[...] = jnp.full(m_sc.shape, NEG, F32)
        l_sc[...] = jnp.zeros(l_sc.shape, F32)
        acc_sc[...] = jnp.zeros(acc_sc.shape, F32)

    lhs = jnp.concatenate([wkt_ref[...], qabs_ref[0][:, :MLA_ROW]], axis=0)

    def attend(rows, mask):
        rb = rows.astype(BF16)
        big = _nt(lhs, rb)
        kn = big[:nk]
        sq = kn * kn
        npe = MLA_HEADS * MLA_NOPE
        ss = jnp.sum(sq[:npe].reshape(MLA_HEADS, MLA_NOPE, sq.shape[1]), axis=1)
        ss = ss + jnp.sum(sq[npe:], axis=0, keepdims=True)
        rinv = lax.rsqrt(ss / MLA_QK + EPS)
        st = big[nk:]
        tiles = [st[MLA_HEADS * tk:MLA_HEADS * (tk + 1)] * rinv for tk in range(dec_seq)]
        sc = jnp.concatenate(tiles, axis=0)
        if mask is not None:
            sc = sc + mask
        _online_update(sc, rb[:, :KV_LORA], m_sc, l_sc, acc_sc, 0)

    copies(s, c, slot, False)
    base = slot * pages * page_rows

    def sbody(i, carry):
        off = pl.multiple_of(base + i * sub, sub)
        attend(buf[pl.ds(off, sub), :], None)
        return carry
    lax.fori_loop(0, pages * page_rows // sub, sbody, 0)

    @pl.when(c == nchunk - 1)
    def _():
        attend(newrows_ref[0], newmask_ref[...])
        o_ref[0] = acc_sc[0] / l_sc[0]


def _mla_decode(page_table, cache_mla3d, qabs, newrows, newmask, w, pages_per_step, sub, dec_seq):
    nseq, npages = page_table.shape
    page_rows = cache_mla3d.shape[1]
    pages = min(pages_per_step, npages)
    assert npages % pages == 0
    nchunk = npages // pages
    sub = min(sub, pages * page_rows)
    rows = MLA_HEADS * dec_seq
    seqmap = lambda s, c, pt: (s, 0, 0)
    return pl.pallas_call(
        functools.partial(_mla_decode_kernel, pages=pages, nchunk=nchunk, page_rows=page_rows, sub=sub,
                          dec_seq=dec_seq),
        out_shape=jax.ShapeDtypeStruct((nseq, rows, LANES), F32),
        grid_spec=pltpu.PrefetchScalarGridSpec(
            num_scalar_prefetch=1,
            grid=(nseq, nchunk),
            in_specs=[pl.BlockSpec(memory_space=pl.ANY),
                      pl.BlockSpec((1, rows, 2 * LANES), seqmap),
                      pl.BlockSpec(w["w_kt_aug"].shape, lambda s, c, pt: (0, 0)),
                      pl.BlockSpec((1, LANES, MLA_ROW), seqmap),
                      pl.BlockSpec(newmask.shape, lambda s, c, pt: (0, 0))],
            out_specs=pl.BlockSpec((1, rows, LANES), seqmap),
            scratch_shapes=[pltpu.VMEM((2 * pages * page_rows, MLA_ROW), F32),
                            pltpu.SemaphoreType.DMA((2,)),
                            pltpu.VMEM((1, rows, LANES), F32),
                            pltpu.VMEM((1, rows, LANES), F32),
                            pltpu.VMEM((1, rows, LANES), F32)]),
        compiler_params=_params(("arbitrary", "arbitrary"), 48),
    )(page_table, cache_mla3d, qabs, w["w_kt_aug"], newrows, newmask)


def _mla_vout_kernel(o_ref, w_ref, out_ref):
    for jp in range(MLA_HEADS // 2):
        pair = o_ref[:, 2 * LANES * jp:2 * LANES * (jp + 1)].astype(BF16)
        out_ref[:, LANES * jp:LANES * (jp + 1)] = _nn(pair, w_ref[jp]).astype(BF16)


def _mla_vout(olat, w):
    t = olat.shape[0]
    return pl.pallas_call(
        _mla_vout_kernel,
        out_shape=jax.ShapeDtypeStruct((t, MLA_HEADS * MLA_VDIM), BF16),
        grid=(1,),
        in_specs=[pl.BlockSpec(olat.shape, lambda i: (0, 0)),
                  pl.BlockSpec(w["w_v_pair"].shape, lambda i: (0, 0, 0))],
        out_specs=pl.BlockSpec((t, MLA_HEADS * MLA_VDIM), lambda i: (0, 0)),
        compiler_params=_params(("arbitrary",), 32),
    )(olat, w["w_v_pair"])


def _half_sumsq_t(kraw, seg_ref):
    ksq = kraw * kraw
    hi = ksq.astype(BF16)
    lo = (ksq - hi.astype(F32)).astype(BF16)
    seg = seg_ref[...]
    return _nt(seg, hi) + _nt(seg, lo)


def _scale_rows(st, rinv8, dec_rows):
    return jnp.concatenate([st[dec_rows * r:dec_rows * (r + 1)] * rinv8 for r in range(NSA_HPG)], axis=0)


def _cmp_decode_kernel(q_ref, kc_ref, vc_ref, bias_ref, cand_ref, gates_ref, o_ref, selb_ref, *, dec_rows):
    s = _nn(q_ref[0], kc_ref[0]) + bias_ref[...]
    vis = bias_ref[...] > 0.5 * NEG
    m = jnp.max(s, axis=1, keepdims=True)
    p = jnp.exp(s - m) * jnp.where(vis, 1.0, 0.0)
    p = p / jnp.maximum(jnp.sum(p, axis=1, keepdims=True), 1e-30)
    o_ref[0] = _nt(p.astype(BF16), vc_ref[0]) * gates_ref[0][:, 0:1]
    imp = p[0:dec_rows]
    for r in range(1, NSA_HPG):
        imp = imp + p[dec_rows * r:dec_rows * (r + 1)]
    sel = _topk_select(jnp.where(cand_ref[...] > 0.5, imp, -1.0), N_SEL - 1)
    selb_ref[0] = jnp.where(sel > 0.0, 0.0, NEG).astype(BF16)


def _cmp_decode(q32, kct, vct, bias, cand, gates32):
    nseq, rows, _ = q32.shape
    nbs = kct.shape[2]
    dec_rows = rows // NSA_HPG
    seqmap = lambda s: (s, 0, 0)
    return pl.pallas_call(
        functools.partial(_cmp_decode_kernel, dec_rows=dec_rows),
        out_shape=[jax.ShapeDtypeStruct((nseq, rows, LANES), F32),
                   jax.ShapeDtypeStruct((nseq, dec_rows, nbs), BF16)],
        grid=(nseq,),
        in_specs=[pl.BlockSpec((1, rows, LANES), seqmap),
                  pl.BlockSpec((1, LANES, nbs), seqmap),
                  pl.BlockSpec((1, LANES, nbs), seqmap),
                  pl.BlockSpec(bias.shape, lambda s: (0, 0)),
                  pl.BlockSpec(cand.shape, lambda s: (0, 0)),
                  pl.BlockSpec((1, rows, LANES), seqmap)],
        out_specs=[pl.BlockSpec((1, rows, LANES), seqmap), pl.BlockSpec((1, dec_rows, nbs), seqmap)],
        compiler_params=_params(("arbitrary",), 32),
    )(q32, kct, vct, bias, cand, gates32)


def _slc_decode_kernel(pt_ref, cache_ref, q_ref, mrows_ref, e4_ref, seg_ref, new_ref, newbias_ref,
                       lastbias_ref, gates_ref, o_ref, buf, sem, m_sc, l_sc, acc_sc, mask_sc,
                       *, pages, nchunk, page_rows, sub, dec_rows):
    s = pl.program_id(0)
    c = pl.program_id(1)
    step = s * nchunk + c
    nsteps = pl.num_programs(0) * nchunk
    slot = step % 2
    nsub = pages * page_rows // sub

    def copies(seq, chunk, slot_, start):
        def one(p, carry):
            page = pt_ref[seq, chunk * pages + p]
            row0 = pl.multiple_of((slot_ * pages + p) * page_rows, page_rows)
            cp = pltpu.make_async_copy(cache_ref.at[page, :, pl.ds(2 * LANES, 2 * LANES)],
                                       buf.at[pl.ds(row0, page_rows), :], sem.at[slot_])
            if start:
                cp.start()
            else:
                cp.wait()
            return carry
        lax.fori_loop(0, pages, one, 0)

    @pl.when(step == 0)
    def _():
        copies(s, c, slot, True)

    @pl.when(step + 1 < nsteps)
    def _():
        nxt = step + 1
        copies(nxt // nchunk, nxt % nchunk, 1 - slot, True)

    @pl.when(c == 0)
    def _():
        m_sc[...] = jnp.full(m_sc.shape, NEG, F32)
        l_sc[...] = jnp.zeros(l_sc.shape, F32)
        acc_sc[...] = jnp.zeros(acc_sc.shape, F32)
        mask_sc[...] = _nn(mrows_ref[0], e4_ref[...])

    q = q_ref[0]

    def attend(kv, extra):
        kraw = kv[:, :LANES]
        st = _nt(q, kraw.astype(BF16))
        rinv = lax.rsqrt(_half_sumsq_t(kraw, seg_ref) / NSA_DH + EPS)
        sc = _scale_rows(st, rinv, dec_rows) + extra
        _online_update(sc, kv[:, LANES:].astype(BF16), m_sc, l_sc, acc_sc, 0)

    copies(s, c, slot, False)
    base = slot * pages * page_rows

    def sbody(i, carry):
        off = pl.multiple_of(base + i * sub, sub)
        gsub = c * nsub + i
        mk = mask_sc[pl.ds(pl.multiple_of(gsub * dec_rows, dec_rows), dec_rows), :]
        mk = jnp.concatenate([mk] * NSA_HPG, axis=0)
        is_last = jnp.where(gsub == nchunk * nsub - 1, 1.0, 0.0)
        attend(buf[pl.ds(off, sub), :], mk + lastbias_ref[...] * is_last)
        return carry
    lax.fori_loop(0, nsub, sbody, 0)

    @pl.when(c == nchunk - 1)
    def _():
        attend(new_ref[0], newbias_ref[...])
        o_ref[0] = acc_sc[0] / l_sc[0] * gates_ref[0][:, 1:2]


def _slc_decode(page_table, cache3d, q32, mrows, new_kv, newbias, lastbias, gates32, w, pages_per_step):
    nseq, npages = page_table.shape
    page_rows = cache3d.shape[1]
    pages = min(pages_per_step, npages)
    assert npages % pages == 0
    nchunk = npages // pages
    sub = 2 * LANES
    rows = q32.shape[1]
    dec_rows = rows // NSA_HPG
    seqmap = lambda s, c, pt: (s, 0, 0)
    const = lambda a: pl.BlockSpec(a.shape, lambda s, c, pt: (0,) * a.ndim)
    return pl.pallas_call(
        functools.partial(_slc_decode_kernel, pages=pages, nchunk=nchunk, page_rows=page_rows, sub=sub,
                          dec_rows=dec_rows),
        out_shape=jax.ShapeDtypeStruct((nseq, rows, LANES), F32),
        grid_spec=pltpu.PrefetchScalarGridSpec(
            num_scalar_prefetch=1,
            grid=(nseq, nchunk),
            in_specs=[pl.BlockSpec(memory_space=pl.ANY),
                      pl.BlockSpec((1, rows, LANES), seqmap),
                      pl.BlockSpec((1,) + mrows.shape[1:], seqmap),
                      const(w["e4"]), const(w["seg8"]),
                      pl.BlockSpec((1, LANES, 2 * LANES), seqmap),
                      const(newbias), const(lastbias),
                      pl.BlockSpec((1, rows, LANES), seqmap)],
            out_specs=pl.BlockSpec((1, rows, LANES), seqmap),
            scratch_shapes=[pltpu.VMEM((2 * pages * page_rows, 2 * LANES), F32),
                            pltpu.SemaphoreType.DMA((2,)),
                            pltpu.VMEM((1, rows, LANES), F32),
                            pltpu.VMEM((1, rows, LANES), F32),
                            pltpu.VMEM((1, rows, LANES), F32),
                            pltpu.VMEM((mrows.shape[1], sub), F32)]),
        compiler_params=_params(("arbitrary", "arbitrary"), 48),
    )(page_table, cache3d, q32, mrows, w["e4"], w["seg8"], new_kv, newbias, lastbias, gates32)


def _win_decode_kernel(q_ref, kv_ref, seg_ref, bias_ref, gates_ref, o_ref, *, dec_rows):
    kv = kv_ref[0]
    kraw = kv[:, :LANES]
    st = _nt(q_ref[0], kraw.astype(BF16))
    rinv = lax.rsqrt(_half_sumsq_t(kraw, seg_ref) / NSA_DH + EPS)
    sc = _scale_rows(st, rinv, dec_rows) + bias_ref[...]
    m = jnp.max(sc, axis=1, keepdims=True)
    p = jnp.exp(sc - m)
    inv = gates_ref[0][:, 2:3] / jnp.sum(p, axis=1, keepdims=True)
    o_ref[0] = _nn(p.astype(BF16), kv[:, LANES:].astype(BF16)) * inv


def _win_decode(q32, wpad, bias, gates32, w):
    nseq, rows, _ = q32.shape
    nkeys = wpad.shape[1]
    seqmap = lambda s: (s, 0, 0)
    return pl.pallas_call(
        functools.partial(_win_decode_kernel, dec_rows=rows // NSA_HPG),
        out_shape=jax.ShapeDtypeStruct((nseq, rows, LANES), F32),
        grid=(nseq,),
        in_specs=[pl.BlockSpec((1, rows, LANES), seqmap),
                  pl.BlockSpec((1, nkeys, 2 * LANES), seqmap),
                  pl.BlockSpec(w["seg8"].shape, lambda s: (0, 0)),
                  pl.BlockSpec(bias.shape, lambda s: (0, 0)),
                  pl.BlockSpec((1, rows, LANES), seqmap)],
        out_specs=pl.BlockSpec((1, rows, LANES), seqmap),
        compiler_params=_params(("arbitrary",), 32),
    )(q32, wpad, w["seg8"], bias, gates32)


def _prep_weights(g_attn, w_in, g_q_lat, w_q_b, g_kv_lat, w_ukv, g_mla_q, g_mla_k, w_up_mla,
                  g_nsa_q, g_k_cmp, g_k_slc, g_k_win,
                  pe_cmp_k, w_cmp_k1, b_cmp_k1, w_cmp_k2, pe_cmp_v, w_cmp_v1, b_cmp_v1, w_cmp_v2,
                  w_up_nsa, w_out, g_ffn, w_router_group, b_router_group, w_router_expert, b_router_expert,
                  w_e_gate, w_e_up, w_e_down):
    d = w_in.shape[0]
    half_rope = MLA_ROPE // 2
    z = lambda *shape: jnp.zeros(shape, F32)
    row = lambda v: v.reshape(1, -1).astype(F32)
    pad_to = lambda v, n: jnp.concatenate([v, z(v.shape[0], n - v.shape[1])], axis=1)

    def rot(cols):
        return jnp.concatenate([-cols[..., half_rope:], cols[..., :half_rope]], axis=-1)

    offs = np.cumsum([Q_LORA, KV_LORA, MLA_ROPE, NSA_HEADS * NSA_DH, 6 * NSA_GROUPS * NSA_DH, 3 * NSA_HEADS])
    c_q, c_kv, kpe, nq, nkv, ngate, mgate = jnp.split(w_in, offs.tolist(), axis=1)
    seg_a = jnp.concatenate([z(d, MLA_NOPE), kpe, z(d, LANES - MLA_QK)], axis=1)
    seg_b = jnp.concatenate([z(d, MLA_NOPE), rot(kpe), z(d, LANES - MLA_QK)], axis=1)
    nq_h = nq.reshape(d, NSA_HEADS, NSA_DH)
    nq_pp = jnp.concatenate([jnp.concatenate([nq_h[:, j], nq_h[:, j + NSA_HPG]], axis=1)
                             for j in range(NSA_HPG)], axis=1)
    w_in_p = jnp.concatenate([c_q, c_kv, seg_a, seg_b, nq_pp, nkv, pad_to(ngate, LANES), mgate], axis=1)

    wq = w_q_b.reshape(Q_LORA, MLA_HEADS, MLA_QK)
    zq = z(Q_LORA, MLA_HEADS, LANES - MLA_QK)
    wq_a = jnp.concatenate([wq, zq], axis=2).reshape(Q_LORA, MLA_HEADS * LANES)
    wq_b = jnp.concatenate([z(Q_LORA, MLA_HEADS, MLA_NOPE), rot(wq[..., MLA_NOPE:]), zq],
                           axis=2).reshape(Q_LORA, MLA_HEADS * LANES)
    wkv = w_ukv.reshape(KV_LORA, MLA_HEADS, MLA_NOPE + MLA_VDIM)
    wk = wkv[..., :MLA_NOPE]
    wv = wkv[..., MLA_NOPE:]
    wk_p = jnp.concatenate([wk, z(KV_LORA, MLA_HEADS, LANES - MLA_NOPE)], axis=2).reshape(KV_LORA, MLA_HEADS * LANES)
    g96 = lambda g: jnp.concatenate([g, z(LANES - MLA_QK)]).reshape(1, LANES)
    pair = lambda g: jnp.concatenate([g, g]).reshape(1, LANES).astype(F32)

    eye_pe = jnp.eye(MLA_ROPE, dtype=F32)
    w_abs = []
    for hd in range(MLA_HEADS):
        top = jnp.concatenate([wk[:, hd, :].T, z(MLA_NOPE, LANES)], axis=1)
        mid = jnp.concatenate([z(MLA_ROPE, KV_LORA), eye_pe, z(MLA_ROPE, LANES - MLA_ROPE)], axis=1)
        w_abs.append(jnp.concatenate([top, mid, z(LANES - MLA_QK, 2 * LANES)], axis=0))
    w_abs = jnp.stack(w_abs)
    wkt = jnp.concatenate([wk.reshape(KV_LORA, MLA_HEADS * MLA_NOPE).T, z(MLA_HEADS * MLA_NOPE, MLA_ROPE)], axis=1)
    w_kt_aug = jnp.concatenate([wkt, jnp.concatenate([z(MLA_ROPE, KV_LORA), eye_pe], axis=1)], axis=0)
    w_v_pair = []
    for jp in range(MLA_HEADS // 2):
        a = jnp.concatenate([wv[:, 2 * jp, :], z(KV_LORA, MLA_VDIM)], axis=1)
        b = jnp.concatenate([z(KV_LORA, MLA_VDIM), wv[:, 2 * jp + 1, :]], axis=1)
        w_v_pair.append(jnp.concatenate([a, b], axis=0))
    w_v_pair = jnp.stack(w_v_pair)

    def cmp_t(w1):
        return jnp.transpose(w1.reshape(CMP_BLOCK, NSA_DH, CMP_HID), (0, 2, 1))
    wk1t, wv1t = cmp_t(w_cmp_k1), cmp_t(w_cmp_v1)
    zz = z(CMP_BLOCK, CMP_HID, NSA_DH)
    cmp_wt = jnp.concatenate([
        jnp.concatenate([wk1t, zz, zz, zz], axis=2),
        jnp.concatenate([zz, wk1t, zz, zz], axis=2),
        jnp.concatenate([zz, zz, wv1t, zz], axis=2),
        jnp.concatenate([zz, zz, zz, wv1t], axis=2)], axis=1)
    cmp_pe = jnp.concatenate([pe_cmp_k, pe_cmp_k, pe_cmp_v, pe_cmp_v], axis=1).reshape(CMP_BLOCK, 1, 2 * LANES)
    cmp_b1 = jnp.concatenate([b_cmp_k1, b_cmp_k1, b_cmp_v1, b_cmp_v1]).reshape(4 * CMP_HID, 1)

    def w2_bd(w2):
        zt = z(NSA_DH, CMP_HID)
        return jnp.concatenate([jnp.concatenate([w2.T, zt], axis=1), jnp.concatenate([zt, w2.T], axis=1)], axis=0)

    perm = np.array([LANES // 2 * (jp + NSA_HPG * half) + dd for jp in range(NSA_HPG)
                     for half in range(NSA_GROUPS) for dd in range(NSA_DH)])
    w_router = pad_to(jnp.concatenate([w_router_expert, w_router_group], axis=1), LANES)
    b_router = pad_to(jnp.concatenate([b_router_expert, b_router_group]).reshape(1, -1), LANES)

    seg8 = np.zeros((8, LANES), np.float32)
    seg8[:4, :HALF] = 1.0
    seg8[4:, HALF:] = 1.0
    e4 = np.zeros((LANES, 2 * LANES), np.float32)
    for i in range(4):
        e4[i, CMP_BLOCK * i:CMP_BLOCK * (i + 1)] = 1.0

    return dict(
        g_attn=row(g_attn), w_in=w_in_p.astype(BF16), g_q_lat=row(g_q_lat),
        w_q=jnp.concatenate([wq_a, wq_b], axis=1).astype(BF16), g_kv_lat=row(g_kv_lat),
        w_k=wk_p.astype(BF16), w_v=wv.reshape(KV_LORA, MLA_HEADS * MLA_VDIM).astype(BF16),
        g_mla_q=g96(g_mla_q), g_mla_k=g96(g_mla_k), g_nsa_q=pair(g_nsa_q), g_k_slc=pair(g_k_slc),
        g_k_win=pair(g_k_win), g_k_cmp_col=jnp.concatenate([g_k_cmp, g_k_cmp]).reshape(LANES, 1).astype(F32),
        w_abs=w_abs.astype(BF16), w_kt_aug=w_kt_aug.astype(BF16), w_v_pair=w_v_pair.astype(BF16),
        cmp_wt=cmp_wt.astype(BF16), cmp_pe=cmp_pe.astype(F32), cmp_b1=cmp_b1.astype(F32),
        cmp_w2k=w2_bd(w_cmp_k2).astype(BF16), cmp_w2v=w2_bd(w_cmp_v2).astype(BF16),
        w_up_mla=w_up_mla.astype(BF16), w_up_nsa=w_up_nsa[perm].astype(BF16), w_out=w_out.astype(BF16),
        g_ffn=row(g_ffn), w_router=w_router.astype(BF16), b_router=b_router.astype(F32),
        w_gu=jnp.concatenate([w_e_gate, w_e_up], axis=2).astype(BF16), w_down=w_e_down.astype(BF16),
        seg8=jnp.asarray(seg8, BF16), e4=jnp.asarray(e4, BF16),
    )


def _rope_table(pos):
    half = MLA_ROPE // 2
    inv = ROPE_BASE ** (-jnp.arange(half, dtype=F32) / half)
    ang = pos.astype(F32)[:, None] * inv
    cos, sin = jnp.cos(ang), jnp.sin(ang)
    t = pos.shape[0]
    one, zero = jnp.ones((t, MLA_NOPE), F32), jnp.zeros((t, MLA_NOPE), F32)
    zpad = jnp.zeros((t, LANES - MLA_QK), F32)
    return jnp.concatenate([one, cos, cos, zpad, zero, sin, sin, zpad], axis=1)


TM_PROJ = 256
TQ_MLA = 512
TQ_CMP = 256
TQ_SLC = 512
TQ_WIN = 256
TM_MERGE = 256
TM_MOE = 1024
CMP_PAGES = 64
DEC_PAGES = 32
MLA_SUB = 512


def kernel(x_prompt, x_sample, cache_mla, cache_nsa_kv, state_nsa_win, page_table, g_attn, w_in, g_q_lat, w_q_b,
           g_kv_lat, w_ukv, g_mla_q, g_mla_k, w_up_mla, g_nsa_q, g_k_cmp, g_k_slc, g_k_win, pe_cmp_k, w_cmp_k1,
           b_cmp_k1, w_cmp_k2, pe_cmp_v, w_cmp_v1, b_cmp_v1, w_cmp_v2, rel_bias, w_up_nsa, w_out, g_ffn,
           w_router_group, b_router_group, w_router_expert, b_router_expert, w_e_gate, w_e_up, w_e_down):
    assert w_in.shape[0] == 1, "single-layer step"
    b, s, d = x_prompt.shape
    nseq, dec_seq, _ = x_sample.shape
    n_pool, page_size = cache_mla.shape[1], cache_mla.shape[2]
    npages = page_table.shape[1]
    past = npages * page_size
    wbuf = state_nsa_win.shape[2]
    nbs = past // CMP_BLOCK
    assert page_size == 2 * CMP_BLOCK and dec_seq < CMP_BLOCK - 1 and s % CMP_BLOCK == 0
    assert past >= 2 * LANES and nbs % 4 == 0

    w = _prep_weights(g_attn[0], w_in[0], g_q_lat[0], w_q_b[0], g_kv_lat[0], w_ukv[0], g_mla_q[0], g_mla_k[0],
                      w_up_mla[0], g_nsa_q[0], g_k_cmp[0], g_k_slc[0], g_k_win[0],
                      pe_cmp_k[0], w_cmp_k1[0], b_cmp_k1[0], w_cmp_k2[0], pe_cmp_v[0], w_cmp_v1[0], b_cmp_v1[0],
                      w_cmp_v2[0], w_up_nsa[0], w_out[0], g_ffn[0], w_router_group[0], b_router_group[0],
                      w_router_expert[0], b_router_expert[0], w_e_gate[0], w_e_up[0], w_e_down[0])
    tq_win = min(TQ_WIN, s)
    bias = _bias_tables(rel_bias, tq_win, past, dec_seq, wbuf, nbs)

    tp = b * s
    pp = _project(x_prompt.reshape(tp, d), _rope_table(jnp.arange(s, dtype=jnp.int32)), w, TM_PROJ, False)
    r3 = lambda a: a.reshape(b, s, a.shape[-1])
    o_mla_p = _mla_prefill(r3(pp["qm"]), r3(pp["km"]), r3(pp["vm"]), TQ_MLA)

    ident = jnp.arange(tp // page_size, dtype=jnp.int32).reshape(b, s // page_size)
    kct_p, vct_p = _compress(ident, pp["nsa4"].reshape(tp // page_size, page_size, 4 * LANES), w, CMP_PAGES)
    tq_cmp = min(TQ_CMP, s)
    ridx = np.arange(tq_cmp) % CMP_BLOCK
    f = bias["f"]
    cols = jnp.concatenate([f[:, 1 + ridx].T, f[:, 1 + CMP_BLOCK + ridx].T,
                            jnp.broadcast_to(f[:, 0][None, :], (tq_cmp, NSA_HEADS)),
                            jnp.zeros((tq_cmp, LANES - 3 * NSA_HEADS), F32)], axis=1)
    gates_p = r3(pp["gates"])
    o_cmp_p, selb_p = _cmp_prefill(r3(pp["nq"]), kct_p, vct_p, cols, gates_p, TQ_CMP)
    o_slc_p = _slc_prefill(r3(pp["nq"]), selb_p, r3(pp["slck"]), r3(pp["slcv"]), bias["td"], bias["ts"],
                           gates_p, TQ_SLC)
    o_win_p = _win_prefill(r3(pp["nq"]), r3(pp["wink"]), r3(pp["winv"]), bias["win"], gates_p, tq_win)
    flat = lambda a: a.reshape(tp, a.shape[-1])
    x1_p, h2_p, comb_p = _merge(x_prompt.reshape(tp, d), flat(o_mla_p), flat(o_cmp_p), flat(o_slc_p),
                                flat(o_win_p), pp["mg"], w, TM_MERGE)
    y_p = _moe(h2_p, x1_p, comb_p, w, TM_MOE).reshape(b, s, d)

    ts_ = nseq * dec_seq
    pos_s = past + jnp.arange(dec_seq, dtype=jnp.int32)
    ps = _project(x_sample.reshape(ts_, d), jnp.tile(_rope_table(pos_s), (nseq, 1)), w, TM_PROJ, True)

    rows_h = MLA_HEADS * dec_seq
    qabs = _mla_qabs(ps["qmf"], w).reshape(nseq, rows_h, 2 * LANES)
    newrows = jnp.concatenate([ps["rows"].reshape(nseq, dec_seq, MLA_ROW),
                               jnp.zeros((nseq, LANES - dec_seq, MLA_ROW), F32)], axis=1)
    tok_r = np.repeat(np.arange(dec_seq), MLA_HEADS)[:, None]
    ncol = np.arange(LANES)[None, :]
    newmask = jnp.asarray(np.where((ncol <= tok_r) & (ncol < dec_seq), 0.0, NEG).astype(np.float32))
    olat = _mla_decode(page_table, cache_mla[0], qabs, newrows, newmask, w, DEC_PAGES, MLA_SUB, dec_seq)
    o_mla_s = _mla_vout(olat.reshape(ts_, MLA_HEADS * LANES), w)

    rows_n = NSA_HEADS * dec_seq
    dec_rows = NSA_GROUPS * dec_seq
    halfmask = jnp.asarray(np.stack([np.arange(LANES) < HALF, np.arange(LANES) >= HALF]).astype(np.float32), BF16)

    def q_rows(nq_pp):
        x = nq_pp.reshape(nseq, dec_seq, NSA_HPG, LANES)
        x = jnp.transpose(x, (0, 2, 1, 3))[:, :, None, :, :] * halfmask[None, None, :, None, :]
        return x.reshape(nseq, rows_n, LANES)

    gsig = ps["gates"].reshape(nseq, dec_seq, LANES)[:, :, :3 * NSA_HEADS]
    gsig = gsig.reshape(nseq, dec_seq, 3, NSA_GROUPS, NSA_HPG)
    gates32 = jnp.transpose(gsig, (0, 4, 3, 1, 2)).reshape(nseq, rows_n, 3)
    gates32 = jnp.concatenate([gates32, jnp.zeros((nseq, rows_n, LANES - 3), F32)], axis=2)

    cache3d = cache_nsa_kv[0].reshape(n_pool, page_size, 4 * LANES)
    kct_s, vct_s = _compress(page_table, cache3d, w, CMP_PAGES)
    cand = np.repeat((np.arange(nbs)[None, :] < ((past + np.arange(dec_seq)) // CMP_BLOCK)[:, None]
                      ).astype(np.float32)[None], NSA_GROUPS, axis=0).reshape(dec_rows, nbs)
    o_cmp32, selb8 = _cmp_decode(q_rows(ps["nq"]), kct_s, vct_s, _sample_rows(bias["s_cmp"]),
                                 jnp.asarray(cand), gates32)
    nsub_total = past // (2 * LANES)
    mrows = jnp.transpose(selb8.reshape(nseq, dec_rows, nsub_total, 4), (0, 2, 1, 3))
    mrows = jnp.concatenate([mrows.reshape(nseq, nsub_total * dec_rows, 4),
                             jnp.zeros((nseq, nsub_total * dec_rows, LANES - 4), BF16)], axis=2)
    new_kv = jnp.concatenate([ps["nsa4"][:, 2 * LANES:].reshape(nseq, dec_seq, 2 * LANES),
                              jnp.zeros((nseq, LANES - dec_seq, 2 * LANES), F32)], axis=1)
    o_slc32 = _slc_decode(page_table, cache3d, q_rows(ps["nq_slc"]), mrows, new_kv,
                          _sample_rows(bias["s_new"]), _sample_rows(bias["s_last"]), gates32, w, DEC_PAGES)
    wcat = jnp.concatenate([state_nsa_win[0].reshape(nseq, wbuf, 2 * LANES),
                            ps["win2"].reshape(nseq, dec_seq, 2 * LANES)], axis=1)
    wpad = jnp.concatenate([wcat, jnp.zeros((nseq, LANES - dec_seq, 2 * LANES), F32)], axis=1)
    o_win32 = _win_decode(q_rows(ps["nq_win"]), wpad, _sample_rows(bias["s_win"]), gates32, w)

    def from_rows(o32):
        x = o32.reshape(nseq, NSA_HPG, NSA_GROUPS, dec_seq, NSA_GROUPS, HALF)
        x = jnp.stack([x[:, :, 0, :, 0, :], x[:, :, 1, :, 1, :]], axis=3)
        return jnp.transpose(x, (0, 2, 1, 3, 4)).reshape(ts_, 4 * LANES)

    x1_s, h2_s, comb_s = _merge(x_sample.reshape(ts_, d), o_mla_s, from_rows(o_cmp32), from_rows(o_slc32),
                                from_rows(o_win32), ps["mg"], w, TM_MERGE)
    y_s = _moe(h2_s, x1_s, comb_s, w, TM_MOE).reshape(nseq, dec_seq, d)

    win_keep = min(WINDOW, s)
    new_mla_prompt = pp["rows"].reshape(1, b, s, MLA_ROW)
    new_mla_sample = ps["rows"].reshape(1, nseq, dec_seq, MLA_ROW)
    new_nsa_prompt = pp["nsa4"].reshape(1, b, s, 4, NSA_GROUPS, NSA_DH)
    new_nsa_sample = ps["nsa4"].reshape(1, nseq, dec_seq, 4, NSA_GROUPS, NSA_DH)
    new_win_prompt = pp["win2"].reshape(b, s, 2, NSA_GROUPS, NSA_DH)[None, :, s - win_keep:]
    new_win_sample = wcat[:, dec_seq:].reshape(1, nseq, wbuf, 2, NSA_GROUPS, NSA_DH)
    return (y_p, y_s, new_mla_prompt, new_mla_sample, new_nsa_prompt, new_nsa_sample, new_win_prompt,
            new_win_sample)
```

```python
import functools
import math

import numpy as np
import jax
import jax.numpy as jnp
from jax import lax
from jax.experimental import pallas as pl
from jax.experimental.pallas import tpu as pltpu

F32 = jnp.float32
BF16 = jnp.bfloat16

MLA_HEADS = 8
MLA_NOPE = 64
MLA_ROPE = 32
MLA_QK = MLA_NOPE + MLA_ROPE
MLA_VDIM = 64
Q_LORA = 256
KV_LORA = 128
MLA_ROW = KV_LORA + MLA_ROPE
ROPE_BASE = 10000.0
NSA_HEADS = 8
NSA_GROUPS = 2
NSA_HPG = NSA_HEADS // NSA_GROUPS
NSA_DH = 64
CMP_BLOCK = 64
CMP_HID = 128
N_SEL = 16
WINDOW = 512
N_BUCKETS = 32
MAX_DISTANCE = 128
N_GROUPS = 4
EXPERTS_PER_GROUP = 8
N_EXPERTS = N_GROUPS * EXPERTS_PER_GROUP
D_EXPERT = 256
EPS = 1e-6
NEG = -1e30
MLA_SCALE = MLA_QK ** -0.5
NSA_SCALE = NSA_DH ** -0.5

LANES = 128
HALF = LANES // 2
SUB = 128
LOOKUP_CHUNK = 8192

O_CQ = 0
O_CKV = O_CQ + Q_LORA
O_PEA = O_CKV + KV_LORA
O_PEB = O_PEA + LANES
O_NQ = O_PEB + LANES
O_NKV = O_NQ + NSA_HEADS * NSA_DH
O_GATE = O_NKV + 6 * NSA_GROUPS * NSA_DH
O_MG = O_GATE + LANES

_NT_DIMS = (((1,), (1,)), ((), ()))


def _nt(a, b):
    return lax.dot_general(a, b, _NT_DIMS, preferred_element_type=F32)


def _nn(a, b):
    return jnp.dot(a, b, preferred_element_type=F32)


def _params(sem, vmem_mb):
    return pltpu.CompilerParams(dimension_semantics=sem, vmem_limit_bytes=vmem_mb * 1024 * 1024)


def _lane_iota(shape):
    return lax.broadcasted_iota(jnp.int32, shape, len(shape) - 1)


def _pair_rms(x, gpair):
    lo = _lane_iota(x.shape) < HALF
    sq = x * x
    s_lo = jnp.sum(jnp.where(lo, sq, 0.0), axis=-1, keepdims=True)
    s_hi = jnp.sum(jnp.where(lo, 0.0, sq), axis=-1, keepdims=True)
    rinv = jnp.where(lo, lax.rsqrt(s_lo / NSA_DH + EPS), lax.rsqrt(s_hi / NSA_DH + EPS))
    return x * rinv * gpair


def _online_update(s, v, m_ref, l_ref, acc_ref, idx, v_transposed=False):
    m_prev = m_ref[idx]
    m_new = jnp.maximum(m_prev, jnp.max(s, axis=1, keepdims=True))
    alpha = jnp.exp(m_prev - m_new)
    p = jnp.exp(s - m_new[:, :1])
    l_ref[idx] = alpha * l_ref[idx] + jnp.sum(p, axis=1, keepdims=True)
    m_ref[idx] = m_new
    pb = p.astype(BF16)
    pv = _nt(pb, v) if v_transposed else _nn(pb, v)
    acc_ref[idx] = acc_ref[idx] * alpha + pv


def _topk_select(scores, n_top):
    lane = _lane_iota(scores.shape).astype(F32)
    big = float(scores.shape[-1])

    def body(_, carry):
        sc, sel = carry
        m = jnp.max(sc, axis=1, keepdims=True)
        first = jnp.min(jnp.where(sc == m, lane, big), axis=1, keepdims=True)
        hit = lane == first
        sel = jnp.where(hit & (m >= 0.0), 1.0, sel)
        sc = jnp.where(hit, -2.0, sc)
        return sc, sel

    _, sel = lax.fori_loop(0, n_top, body, (scores, jnp.zeros_like(scores)))
    return sel


def _bias_lookup_kernel(tbl_ref, bucket_ref, neg_ref, out_ref):
    tbl = tbl_ref[...]
    b = bucket_ref[...]
    acc = jnp.zeros(out_ref.shape, F32)
    for k in range(N_BUCKETS):
        acc = jnp.where(b == k, tbl[:, k:k + 1], acc)
    out_ref[...] = acc - tbl[:, N_BUCKETS - 1:N_BUCKETS] + neg_ref[...]


def _bias_lookup(table_t, bucket, neg):
    n = bucket.shape[1]
    assert n % LOOKUP_CHUNK == 0
    return pl.pallas_call(
        _bias_lookup_kernel,
        out_shape=jax.ShapeDtypeStruct((NSA_HEADS, n), F32),
        grid=(n // LOOKUP_CHUNK,),
        in_specs=[pl.BlockSpec((NSA_HEADS, N_BUCKETS), lambda i: (0, 0)),
                  pl.BlockSpec((1, LOOKUP_CHUNK), lambda i: (0, i)),
                  pl.BlockSpec((1, LOOKUP_CHUNK), lambda i: (0, i))],
        out_specs=pl.BlockSpec((NSA_HEADS, LOOKUP_CHUNK), lambda i: (0, i)),
        name="bias_lookup",
        compiler_params=_params(("arbitrary",), 32),
    )(table_t, bucket, neg)


def _t5_bucket_np(dist):
    n = np.maximum(dist, 0)
    max_exact = N_BUCKETS // 2
    log_ratio = np.log(np.maximum(n, 1).astype(np.float32) / np.float32(max_exact)) / np.float32(
        math.log(MAX_DISTANCE / max_exact))
    large = max_exact + (log_ratio.astype(np.float32) * (N_BUCKETS - max_exact)).astype(np.int32)
    return np.where(n < max_exact, n, np.minimum(large, N_BUCKETS - 1)).astype(np.int32)


def _bias_tables(rel_bias, tq_win, past, dec_seq, wbuf, nbs, sub_keys):
    pieces = {}

    def add(name, dist, valid):
        dist = np.asarray(dist)
        pieces[name] = (dist.shape, _t5_bucket_np(dist).reshape(-1),
                        np.where(np.asarray(valid), 0.0, NEG).astype(np.float32).reshape(-1))

    d = np.arange(2 * SUB)
    add("f", d, np.ones_like(d, bool))
    i = np.arange(SUB)[:, None]
    j = np.arange(SUB)[None, :]
    add("td", i - j, i >= j)
    add("ts", SUB + i - j, np.ones((SUB, SUB), bool))
    nkb = WINDOW // tq_win + 1
    iw = np.arange(tq_win)[:, None]
    cw = np.arange(nkb * tq_win)[None, :]
    dw = WINDOW + iw - cw
    add("win", dw, (dw >= 0) & (dw < WINDOW))
    tok = np.arange(dec_seq)[:, None]
    pos = past + tok
    jc = np.arange(nbs)[None, :]
    dc = pos - (jc * CMP_BLOCK + CMP_BLOCK - 1)
    add("s_cmp", dc, dc >= 0)
    wpad = wbuf + LANES
    w = np.arange(wpad)[None, :]
    dsw = wbuf + tok - w
    add("s_win", dsw, (dsw >= 0) & (dsw < WINDOW) & (w < wbuf + dec_seq))
    n = np.arange(LANES)[None, :]
    add("s_new", tok - n, (n <= tok) & (n < dec_seq))
    kl = np.arange(sub_keys)[None, :]
    add("s_last", pos - (past - sub_keys + kl), np.ones((dec_seq, sub_keys), bool))

    total = sum(v[1].size for v in pieces.values())
    padded = -(-total // LOOKUP_CHUNK) * LOOKUP_CHUNK
    bucket = np.zeros((1, padded), np.int32)
    neg = np.zeros((1, padded), np.float32)
    off = 0
    spans = {}
    for name, (shape, b, ng) in pieces.items():
        bucket[0, off:off + b.size] = b
        neg[0, off:off + b.size] = ng
        spans[name] = (off, shape)
        off += b.size
    flat = _bias_lookup(rel_bias.T.astype(F32), jnp.asarray(bucket), jnp.asarray(neg))
    out = {}
    for name, (o, shape) in spans.items():
        out[name] = flat[:, o:o + int(np.prod(shape))].reshape((NSA_HEADS,) + tuple(shape))
    return out


def _sample_rows(tile):
    h, t, w = tile.shape
    x = tile.reshape(NSA_GROUPS, NSA_HPG, t, w)
    return jnp.transpose(x, (1, 0, 2, 3)).reshape(h * t, w)


def _proj_kernel(x_ref, cs_ref, gattn_ref, win_ref, gq_ref, wq_ref, gkv_ref, wk_ref, wv_ref,
                 gmq_ref, gmk_ref, gnq_ref, gslc_ref, gwin_ref,
                 rows_ref, qm_ref, km_ref, vm_ref, nq_ref, nsa4_ref, win2_ref,
                 slck_ref, slcv_ref, wink_ref, winv_ref, gates_ref, mg_ref, *extra, d_model):
    x = x_ref[...]
    ms = jnp.mean(x * x, axis=-1, keepdims=True)
    h = (x * lax.rsqrt(ms + EPS) * gattn_ref[...]).astype(BF16)

    def seg(o, n):
        return _nn(h, win_ref[:, o:o + n])

    cs = cs_ref[...]
    cos_t = cs[:, :LANES]
    sin_t = cs[:, LANES:]

    ckv = seg(O_CKV, KV_LORA)
    lat = ckv * lax.rsqrt(jnp.mean(ckv * ckv, axis=-1, keepdims=True) + EPS) * gkv_ref[...]
    kpe = seg(O_PEA, LANES) * cos_t + seg(O_PEB, LANES) * sin_t
    rows_ref[:, 0:KV_LORA] = lat
    rows_ref[:, KV_LORA:MLA_ROW] = pltpu.roll(kpe, HALF, 1)[:, 0:MLA_ROPE]

    cq = seg(O_CQ, Q_LORA)
    cqn = (cq * lax.rsqrt(jnp.mean(cq * cq, axis=-1, keepdims=True) + EPS) * gq_ref[...]).astype(BF16)
    qab = _nn(cqn, wq_ref[...])
    lat_b = lat.astype(BF16)
    kall = _nn(lat_b, wk_ref[...])
    nh = MLA_HEADS * LANES
    for hd in range(MLA_HEADS):
        sl = slice(LANES * hd, LANES * (hd + 1))
        qh = qab[:, sl] * cos_t + qab[:, nh + LANES * hd:nh + LANES * (hd + 1)] * sin_t
        qn = qh * lax.rsqrt(jnp.sum(qh * qh, axis=-1, keepdims=True) / MLA_QK + EPS) * gmq_ref[...] * MLA_SCALE
        qm_ref[:, sl] = qn.astype(BF16)
        if extra:
            extra[0][:, sl] = qn
        kh = kall[:, sl] + kpe
        kn = kh * lax.rsqrt(jnp.sum(kh * kh, axis=-1, keepdims=True) / MLA_QK + EPS) * gmk_ref[...]
        km_ref[:, sl] = kn.astype(BF16)
    vm_ref[...] = _nn(lat_b, wv_ref[...]).astype(BF16)

    nq = seg(O_NQ, NSA_HEADS * NSA_DH)
    for j in range(NSA_HPG):
        sl = slice(LANES * j, LANES * (j + 1))
        qj = _pair_rms(nq[:, sl], gnq_ref[...]) * NSA_SCALE
        nq_ref[:, sl] = qj.astype(BF16)
        if extra:
            extra[1][:, sl] = (qj * gslc_ref[...]).astype(BF16)
            extra[2][:, sl] = (qj * gwin_ref[...]).astype(BF16)
    nkv = seg(O_NKV, 6 * LANES)
    nsa4_ref[...] = nkv[:, :4 * LANES]
    win2_ref[...] = nkv[:, 4 * LANES:]
    slck_ref[...] = _pair_rms(nkv[:, 2 * LANES:3 * LANES], gslc_ref[...]).astype(BF16)
    slcv_ref[...] = nkv[:, 3 * LANES:4 * LANES].astype(BF16)
    wink_ref[...] = _pair_rms(nkv[:, 4 * LANES:5 * LANES], gwin_ref[...]).astype(BF16)
    winv_ref[...] = nkv[:, 5 * LANES:].astype(BF16)
    gates_ref[...] = jax.nn.sigmoid(seg(O_GATE, LANES))
    mg_ref[...] = jax.nn.sigmoid(seg(O_MG, 2 * d_model))


def _project(x2d, cs, w, tm, sample):
    t, d = x2d.shape
    tm = min(tm, t)
    assert t % tm == 0 and cs.shape[0] % tm == 0
    n_cs = cs.shape[0] // tm
    full = lambda a: pl.BlockSpec(a.shape, lambda i: (0,) * a.ndim)
    tok = lambda n: pl.BlockSpec((tm, n), lambda i: (i, 0))
    weights = [w["g_attn"], w["w_in"], w["g_q_lat"], w["w_q"], w["g_kv_lat"], w["w_k"], w["w_v"],
               w["g_mla_q"], w["g_mla_k"], w["g_nsa_q"], w["g_k_slc"], w["g_k_win"]]
    outs = [("rows", MLA_ROW, F32), ("qm", 8 * LANES, BF16), ("km", 8 * LANES, BF16), ("vm", 4 * LANES, BF16),
            ("nq", 4 * LANES, BF16), ("nsa4", 4 * LANES, F32), ("win2", 2 * LANES, F32),
            ("slck", LANES, BF16), ("slcv", LANES, BF16), ("wink", LANES, BF16), ("winv", LANES, BF16),
            ("gates", LANES, F32), ("mg", 2 * d, F32)]
    if sample:
        outs += [("qmf", 8 * LANES, F32), ("nq_slc", 4 * LANES, BF16), ("nq_win", 4 * LANES, BF16)]
    res = pl.pallas_call(
        functools.partial(_proj_kernel, d_model=d),
        out_shape=[jax.ShapeDtypeStruct((t, n), dt) for _, n, dt in outs],
        grid=(t // tm,),
        in_specs=[tok(d), pl.BlockSpec((tm, 2 * LANES), lambda i: (i % n_cs, 0))] + [full(a) for a in weights],
        out_specs=[tok(n) for _, n, _ in outs],
        name="project_sample" if sample else "project_prompt",
        compiler_params=_params(("arbitrary",), 56),
    )(x2d, cs, *weights)
    return {name: r for (name, _, _), r in zip(outs, res)}


def _tri_tables(n):
    qi = np.concatenate([np.full(q + 1, q) for q in range(n)]).astype(np.int32)
    ki = np.concatenate([np.arange(q + 1) for q in range(n)]).astype(np.int32)
    return jnp.asarray(qi), jnp.asarray(ki)


def _mla_prefill_kernel(qt_ref, kt_ref, q_ref, k_ref, v_ref, o_ref, m_sc, l_sc, acc_sc, *, tq):
    qi = qt_ref[pl.program_id(2)]
    ki = kt_ref[pl.program_id(2)]

    @pl.when(ki == 0)
    def _():
        m_sc[...] = jnp.full(m_sc.shape, NEG, F32)
        l_sc[...] = jnp.zeros(l_sc.shape, F32)
        acc_sc[...] = jnp.zeros(acc_sc.shape, F32)

    def body(diag):
        v = v_ref[0]
        for hh in range(2):
            sl = slice(LANES * hh, LANES * (hh + 1))
            s = _nt(q_ref[0, :, sl], k_ref[0, :, sl])
            if diag:
                row = lax.broadcasted_iota(jnp.int32, s.shape, 0)
                col = lax.broadcasted_iota(jnp.int32, s.shape, 1)
                s = jnp.where(col <= row, s, NEG)
            _online_update(s, v, m_sc, l_sc, acc_sc, hh)

    @pl.when(ki < qi)
    def _():
        body(False)

    @pl.when(ki == qi)
    def _():
        body(True)
        lo = _lane_iota((tq, LANES)) < HALF
        o_ref[0] = jnp.where(lo, acc_sc[0] / l_sc[0], acc_sc[1] / l_sc[1]).astype(BF16)


def _mla_prefill(qm, km, vm, tq):
    b, s, _ = qm.shape
    tq = min(tq, s)
    qt, kt = _tri_tables(s // tq)
    qmap = lambda bb, j, t, qt_, kt_: (bb, qt_[t], j)
    kmap = lambda bb, j, t, qt_, kt_: (bb, kt_[t], j)
    return pl.pallas_call(
        functools.partial(_mla_prefill_kernel, tq=tq),
        out_shape=jax.ShapeDtypeStruct((b, s, MLA_HEADS * MLA_VDIM), BF16),
        grid_spec=pltpu.PrefetchScalarGridSpec(
            num_scalar_prefetch=2,
            grid=(b, MLA_HEADS // 2, qt.shape[0]),
            in_specs=[pl.BlockSpec((1, tq, 2 * LANES), qmap),
                      pl.BlockSpec((1, tq, 2 * LANES), kmap),
                      pl.BlockSpec((1, tq, LANES), kmap)],
            out_specs=pl.BlockSpec((1, tq, LANES), qmap),
            scratch_shapes=[pltpu.VMEM((2, tq, LANES), F32)] * 3),
        name="mla_prefill",
        compiler_params=_params(("arbitrary",) * 3, 48),
    )(qt, kt, qm, km, vm)


def _compress_kernel(pt_ref, cache_ref, wt_ref, pe_ref, b1_ref, w2k_ref, w2v_ref, gk_ref,
                     kc_ref, vc_ref, bufk, bufv, sem, acc, *, pages, nchunk, lane0):
    s = pl.program_id(0)
    c = pl.program_id(1)
    step = s * nchunk + c
    nsteps = pl.num_programs(0) * nchunk
    slot = step % 2
    page_rows = 2 * CMP_BLOCK

    def copies(seq, chunk, slot_, start):
        def one(p, carry):
            page = pt_ref[seq, chunk * pages + p]
            row0 = pl.multiple_of((slot_ * pages + p) * page_rows, page_rows)
            for kv, dst in enumerate((bufk, bufv)):
                cp = pltpu.make_async_copy(cache_ref.at[page, :, pl.ds(lane0 + LANES * kv, LANES)],
                                           dst.at[pl.ds(row0, page_rows), :], sem.at[slot_, kv])
                if start:
                    cp.start()
                else:
                    cp.wait()
            return carry
        lax.fori_loop(0, pages, one, 0)

    @pl.when(step == 0)
    def _():
        copies(s, c, slot, True)

    @pl.when(step + 1 < nsteps)
    def _():
        nxt = step + 1
        copies(nxt // nchunk, nxt % nchunk, 1 - slot, True)

    copies(s, c, slot, False)
    acc[...] = jnp.zeros(acc.shape, F32)
    base = slot * pages * page_rows

    def rbody(r, carry):
        def rows_of(src):
            return jnp.concatenate([src[pl.ds(base + r, pages, stride=page_rows), :],
                                    src[pl.ds(base + CMP_BLOCK + r, pages, stride=page_rows), :]], axis=0)
        xr = (jnp.concatenate([rows_of(bufk), rows_of(bufv)], axis=1) + pe_ref[r]).astype(BF16)
        acc[...] += _nt(wt_ref[r], xr)
        return carry
    lax.fori_loop(0, CMP_BLOCK, rbody, 0)

    hid = jax.nn.gelu(acc[...] + b1_ref[...])
    kc = _nn(w2k_ref[...], hid[:2 * CMP_HID].astype(BF16))
    vc = _nn(w2v_ref[...], hid[2 * CMP_HID:].astype(BF16))
    gk = gk_ref[...]
    parts = []
    for g in range(NSA_GROUPS):
        part = kc[NSA_DH * g:NSA_DH * (g + 1)]
        ss = jnp.sum(part * part, axis=0, keepdims=True)
        parts.append(part * lax.rsqrt(ss / NSA_DH + EPS) * gk[NSA_DH * g:NSA_DH * (g + 1)])
    kc_ref[0] = jnp.concatenate(parts, axis=0).astype(BF16)
    vc_ref[0] = vc.astype(BF16)


def _compress(page_table, cache3d, w, pages_per_step):
    nseq, npages = page_table.shape
    pages = min(pages_per_step, npages)
    assert npages % pages == 0
    nchunk = npages // pages
    nblk = 2 * pages
    full = lambda a: pl.BlockSpec(a.shape, lambda s, c, pt: (0,) * a.ndim)
    consts = [w["cmp_wt"], w["cmp_pe"], w["cmp_b1"], w["cmp_w2k"], w["cmp_w2v"], w["g_k_cmp_col"]]
    out_spec = pl.BlockSpec((1, LANES, nblk), lambda s, c, pt: (s, 0, c))
    kc, vc = pl.pallas_call(
        functools.partial(_compress_kernel, pages=pages, nchunk=nchunk, lane0=0),
        out_shape=[jax.ShapeDtypeStruct((nseq, LANES, nchunk * nblk), BF16)] * 2,
        grid_spec=pltpu.PrefetchScalarGridSpec(
            num_scalar_prefetch=1,
            grid=(nseq, nchunk),
            in_specs=[pl.BlockSpec(memory_space=pl.ANY)] + [full(a) for a in consts],
            out_specs=[out_spec, out_spec],
            scratch_shapes=[pltpu.VMEM((2 * pages * 2 * CMP_BLOCK, LANES), F32),
                            pltpu.VMEM((2 * pages * 2 * CMP_BLOCK, LANES), F32),
                            pltpu.SemaphoreType.DMA((2, 2)),
                            pltpu.VMEM((4 * CMP_HID, nblk), F32)]),
        name="compress_prompt",
        compiler_params=_params(("arbitrary", "arbitrary"), 56),
    )(page_table, cache3d, *consts)

    def unpermute(a):
        a = a.reshape(nseq, LANES, nchunk, 2, pages)
        return jnp.transpose(a, (0, 1, 2, 4, 3)).reshape(nseq, LANES, nchunk * nblk)
    return unpermute(kc), unpermute(vc)


def _cmp_prefill_kernel(q_ref, kc_ref, vc_ref, cols_ref, gates_ref, o_ref, selb_ref, *, tq, nb, nbp):
    qi = pl.program_id(1)
    kc = kc_ref[0]
    vc = vc_ref[0]
    cols = cols_ref[...]
    gates = gates_ref[0]
    t = lax.broadcasted_iota(jnp.int32, (tq, nb), 0) + qi * tq
    j = lax.broadcasted_iota(jnp.int32, (tq, nb), 1)
    cb = t // CMP_BLOCK
    r = t % CMP_BLOCK
    u = cb - j
    vis = (u >= 1) | ((u == 0) & (r == CMP_BLOCK - 1))
    visf = jnp.where(vis, 1.0, 0.0)
    lo = _lane_iota((tq, LANES)) < HALF
    imp = [jnp.zeros((tq, nb), F32), jnp.zeros((tq, nb), F32)]
    outs = []
    for jp in range(NSA_HPG):
        qv = q_ref[0, :, LANES * jp:LANES * (jp + 1)]
        halves = []
        for half in range(NSA_GROUPS):
            hd = jp + NSA_HPG * half
            qh = jnp.where(lo if half == 0 else jnp.logical_not(lo), qv, jnp.zeros_like(qv))
            s = _nn(qh, kc)
            bias = jnp.where(u == 1, cols[:, hd:hd + 1],
                             jnp.where(u == 2, cols[:, 8 + hd:9 + hd],
                                       jnp.where(u == 0, cols[:, 16 + hd:17 + hd], 0.0)))
            s = jnp.where(vis, s + bias, NEG)
            m = jnp.max(s, axis=1, keepdims=True)
            p = jnp.exp(s - m) * visf
            p = p / jnp.maximum(jnp.sum(p, axis=1, keepdims=True), 1e-30)
            imp[half] = imp[half] + p
            halves.append(_nt(p.astype(BF16), vc) * gates[:, hd:hd + 1])
        outs.append(jnp.where(lo, halves[0], halves[1]))
    o_ref[0] = jnp.concatenate(outs, axis=1)
    cand = j < cb
    scores = jnp.concatenate([jnp.where(cand, imp[g], -1.0) for g in range(NSA_GROUPS)], axis=0)
    sel_all = _topk_select(scores, N_SEL - 1)
    sels = []
    for g in range(NSA_GROUPS):
        sb = jnp.where((sel_all[tq * g:tq * (g + 1)] > 0.0) | (j == cb), 0.0, NEG)
        if nbp > nb:
            sb = jnp.concatenate([sb, jnp.full((tq, nbp - nb), NEG, F32)], axis=1)
        sels.append(sb)
    selb_ref[0] = jnp.concatenate(sels, axis=1).astype(BF16)


def _cmp_prefill(nq, kct, vct, cols, gates, tq):
    b, s, _ = nq.shape
    tq = min(tq, s)
    nb = kct.shape[2]
    nbp = -(-nb // LANES) * LANES
    return pl.pallas_call(
        functools.partial(_cmp_prefill_kernel, tq=tq, nb=nb, nbp=nbp),
        out_shape=[jax.ShapeDtypeStruct((b, s, 4 * LANES), F32),
                   jax.ShapeDtypeStruct((b, s, NSA_GROUPS * nbp), BF16)],
        grid=(b, s // tq),
        in_specs=[pl.BlockSpec((1, tq, 4 * LANES), lambda bb, qi: (bb, qi, 0)),
                  pl.BlockSpec((1, LANES, nb), lambda bb, qi: (bb, 0, 0)),
                  pl.BlockSpec((1, LANES, nb), lambda bb, qi: (bb, 0, 0)),
                  pl.BlockSpec((tq, LANES), lambda bb, qi: (0, 0)),
                  pl.BlockSpec((1, tq, LANES), lambda bb, qi: (bb, qi, 0))],
        out_specs=[pl.BlockSpec((1, tq, 4 * LANES), lambda bb, qi: (bb, qi, 0)),
                   pl.BlockSpec((1, tq, NSA_GROUPS * nbp), lambda bb, qi: (bb, qi, 0))],
        name="cmp_prefill",
        compiler_params=_params(("arbitrary", "arbitrary"), 48),
    )(nq, kct, vct, cols, gates)


def _slc_prefill_kernel(qt_ref, kt_ref, q_ref, selb_ref, k_ref, v_ref, td_ref, ts_ref, gates_ref, o_ref,
                        m_sc, l_sc, acc_sc, *, tq, nbp):
    jp = pl.program_id(1)
    qi = qt_ref[pl.program_id(2)]
    ki = kt_ref[pl.program_id(2)]
    nsub = tq // SUB

    @pl.when(ki == 0)
    def _():
        m_sc[...] = jnp.full(m_sc.shape, NEG, F32)
        l_sc[...] = jnp.zeros(l_sc.shape, F32)
        acc_sc[...] = jnp.zeros(acc_sc.shape, F32)

    def add_bias(s, half, kind):
        if kind == 0:
            return s
        td = td_ref[jp + NSA_HPG * half]
        ts = ts_ref[jp + NSA_HPG * half]
        rows = []
        for a in range(nsub):
            row = s[SUB * a:SUB * (a + 1), :]
            pcs = [row[:, SUB * c:SUB * (c + 1)] for c in range(nsub)]
            if kind == 2:
                pcs[a] = pcs[a] + td
                if a >= 1:
                    pcs[a - 1] = pcs[a - 1] + ts
            elif a == 0:
                pcs[nsub - 1] = pcs[nsub - 1] + ts
            rows.append(jnp.concatenate(pcs, axis=1) if nsub > 1 else pcs[0])
        return jnp.concatenate(rows, axis=0) if nsub > 1 else rows[0]

    def body(kind):
        k = k_ref[0]
        v = v_ref[0]
        kpos = lax.broadcasted_iota(jnp.int32, (tq, nbp), 0) + ki * tq
        blk = lax.broadcasted_iota(jnp.int32, (tq, nbp), 1)
        onehot = jnp.where(kpos // CMP_BLOCK == blk, 1.0, 0.0).astype(BF16)
        kaug = jnp.concatenate([k, onehot], axis=1)
        lo = _lane_iota((tq, LANES)) < HALF
        qv = q_ref[0]
        for half in range(NSA_GROUPS):
            qh = jnp.where(lo if half == 0 else jnp.logical_not(lo), qv, jnp.zeros_like(qv))
            qaug = jnp.concatenate([qh, selb_ref[0, :, nbp * half:nbp * (half + 1)]], axis=1)
            _online_update(add_bias(_nt(qaug, kaug), half, kind), v, m_sc, l_sc, acc_sc, half)

    @pl.when(ki < qi - 1)
    def _():
        body(0)

    @pl.when(ki == qi - 1)
    def _():
        body(1)

    @pl.when(ki == qi)
    def _():
        body(2)
        gates = gates_ref[0]
        lane = _lane_iota(gates.shape)
        g0 = jnp.sum(jnp.where(lane == NSA_HEADS + jp, gates, 0.0), axis=1, keepdims=True)
        g1 = jnp.sum(jnp.where(lane == NSA_HEADS + NSA_HPG + jp, gates, 0.0), axis=1, keepdims=True)
        lo = _lane_iota((tq, LANES)) < HALF
        o_ref[0] = jnp.where(lo, acc_sc[0] / l_sc[0] * g0, acc_sc[1] / l_sc[1] * g1)


def _slc_prefill(nq, selb, slck, slcv, td, ts, gates, tq):
    b, s, _ = nq.shape
    tq = min(tq, s)
    nbp = selb.shape[2] // NSA_GROUPS
    qt, kt = _tri_tables(s // tq)
    qmap = lambda bb, j, t, qt_, kt_: (bb, qt_[t], 0)
    kmap = lambda bb, j, t, qt_, kt_: (bb, kt_[t], 0)
    const = lambda bb, j, t, qt_, kt_: (0, 0, 0)
    return pl.pallas_call(
        functools.partial(_slc_prefill_kernel, tq=tq, nbp=nbp),
        out_shape=jax.ShapeDtypeStruct((b, s, 4 * LANES), F32),
        grid_spec=pltpu.PrefetchScalarGridSpec(
            num_scalar_prefetch=2,
            grid=(b, NSA_HPG, qt.shape[0]),
            in_specs=[pl.BlockSpec((1, tq, LANES), lambda bb, j, t, qt_, kt_: (bb, qt_[t], j)),
                      pl.BlockSpec((1, tq, NSA_GROUPS * nbp), qmap),
                      pl.BlockSpec((1, tq, LANES), kmap),
                      pl.BlockSpec((1, tq, LANES), kmap),
                      pl.BlockSpec((NSA_HEADS, SUB, SUB), const),
                      pl.BlockSpec((NSA_HEADS, SUB, SUB), const),
                      pl.BlockSpec((1, tq, LANES), qmap)],
            out_specs=pl.BlockSpec((1, tq, LANES), lambda bb, j, t, qt_, kt_: (bb, qt_[t], j)),
            scratch_shapes=[pltpu.VMEM((NSA_GROUPS, tq, LANES), F32)] * 3),
        name="slc_prefill",
        compiler_params=_params(("arbitrary",) * 3, 48),
    )(qt, kt, nq, selb, slck, slcv, td, ts, gates)


def _win_prefill_kernel(q_ref, *refs, tq, nkb):
    k_refs = refs[:nkb]
    v_refs = refs[nkb:2 * nkb]
    bias_ref, gates_ref, o_ref = refs[2 * nkb:]
    qi = pl.program_id(1)
    kcat = jnp.concatenate([kr[0] for kr in k_refs], axis=0)
    vcat = jnp.concatenate([vr[0] for vr in v_refs], axis=0)
    gates = gates_ref[0]
    col_blk = lax.broadcasted_iota(jnp.int32, (tq, nkb * tq), 1) // tq
    in_seq = col_blk + qi >= nkb - 1
    lo = _lane_iota((tq, LANES)) < HALF
    outs = []
    for jp in range(NSA_HPG):
        qv = q_ref[0, :, LANES * jp:LANES * (jp + 1)]
        halves = []
        for half in range(NSA_GROUPS):
            hd = jp + NSA_HPG * half
            qh = jnp.where(lo if half == 0 else jnp.logical_not(lo), qv, jnp.zeros_like(qv))
            s = jnp.where(in_seq, _nt(qh, kcat) + bias_ref[hd], NEG)
            m = jnp.max(s, axis=1, keepdims=True)
            p = jnp.exp(s - m)
            inv = gates[:, 16 + hd:17 + hd] / jnp.sum(p, axis=1, keepdims=True)
            halves.append(_nn(p.astype(BF16), vcat) * inv)
        outs.append(jnp.where(lo, halves[0], halves[1]))
    o_ref[0] = jnp.concatenate(outs, axis=1)


def _win_prefill(nq, wink, winv, bias, gates, tq):
    b, s, _ = nq.shape
    nkb = bias.shape[2] // bias.shape[1]
    assert bias.shape[1] == tq and s % tq == 0
    qmap = lambda bb, qi: (bb, qi, 0)
    kmaps = [(lambda bb, qi, o=o: (bb, jnp.maximum(qi - o, 0), 0)) for o in range(nkb - 1, -1, -1)]
    kspecs = [pl.BlockSpec((1, tq, LANES), km) for km in kmaps]
    return pl.pallas_call(
        functools.partial(_win_prefill_kernel, tq=tq, nkb=nkb),
        out_shape=jax.ShapeDtypeStruct((b, s, 4 * LANES), F32),
        grid=(b, s // tq),
        in_specs=[pl.BlockSpec((1, tq, 4 * LANES), qmap)] + kspecs + kspecs +
                 [pl.BlockSpec(bias.shape, lambda bb, qi: (0, 0, 0)), pl.BlockSpec((1, tq, LANES), qmap)],
        out_specs=pl.BlockSpec((1, tq, 4 * LANES), qmap),
        name="win_prefill",
        compiler_params=_params(("arbitrary", "arbitrary"), 56),
    )(nq, *([wink] * nkb), *([winv] * nkb), bias, gates)


def _merge_kernel(x_ref, omla_ref, oc_ref, os_ref, ow_ref, mg_ref, wum_ref, wun_ref, wout_ref,
                  gffn_ref, wr_ref, br_ref, x1_ref, h2_ref, comb_ref, *, d_model):
    o_nsa = (oc_ref[...] + os_ref[...]) + ow_ref[...]
    ym = _nn(omla_ref[...], wum_ref[...])
    yn = _nn(o_nsa.astype(BF16), wun_ref[...])
    mg = mg_ref[...]
    y = mg[:, :d_model] * ym + mg[:, d_model:] * yn
    x1 = x_ref[...] + _nn(y.astype(BF16), wout_ref[...])
    x1_ref[...] = x1
    h2 = x1 * lax.rsqrt(jnp.mean(x1 * x1, axis=-1, keepdims=True) + EPS) * gffn_ref[...]
    h2b = h2.astype(BF16)
    h2_ref[...] = h2b
    logits = _nn(h2b, wr_ref[...]) + br_ref[...]
    lane = _lane_iota(logits.shape).astype(F32)
    is_g = (lane >= N_EXPERTS) & (lane < N_EXPERTS + N_GROUPS)
    gl = jnp.where(is_g, logits, NEG)
    gmax = jnp.max(gl, axis=1, keepdims=True)
    gsel = jnp.min(jnp.where(gl == gmax, lane, 1e9), axis=1, keepdims=True) - N_EXPERTS
    gw = 1.0 / jnp.sum(jnp.exp(gl - gmax), axis=1, keepdims=True)
    ing = (lane >= gsel * EXPERTS_PER_GROUP) & (lane < (gsel + 1.0) * EXPERTS_PER_GROUP)
    el = jnp.where(ing, logits, NEG)
    ex = jnp.exp(el - jnp.max(el, axis=1, keepdims=True))
    p = ex / jnp.sum(ex, axis=1, keepdims=True)
    pin = jnp.where(ing, p, -1.0)
    p1 = jnp.max(pin, axis=1, keepdims=True)
    i1 = jnp.min(jnp.where(pin == p1, lane, 1e9), axis=1, keepdims=True)
    prest = jnp.where(lane == i1, -1.0, pin)
    p2 = jnp.max(prest, axis=1, keepdims=True)
    i2 = jnp.min(jnp.where(prest == p2, lane, 1e9), axis=1, keepdims=True)
    tot = p1 + p2
    comb_ref[...] = jnp.where(lane == i1, gw * p1 / tot, jnp.where(lane == i2, gw * p2 / tot, 0.0))


def _merge(x2d, omla, oc, osl, ow, mg, w, tm):
    t, d = x2d.shape
    tm = min(tm, t)
    full = lambda a: pl.BlockSpec(a.shape, lambda i: (0,) * a.ndim)
    tok = lambda n: pl.BlockSpec((tm, n), lambda i: (i, 0))
    weights = [w["w_up_mla"], w["w_up_nsa"], w["w_out"], w["g_ffn"], w["w_router"], w["b_router"]]
    return pl.pallas_call(
        functools.partial(_merge_kernel, d_model=d),
        out_shape=[jax.ShapeDtypeStruct((t, d), F32), jax.ShapeDtypeStruct((t, d), BF16),
                   jax.ShapeDtypeStruct((t, LANES), F32)],
        grid=(t // tm,),
        in_specs=[tok(d), tok(4 * LANES), tok(4 * LANES), tok(4 * LANES), tok(4 * LANES), tok(2 * d)] +
                 [full(a) for a in weights],
        out_specs=[tok(d), tok(d), tok(LANES)],
        name="merge_router",
        compiler_params=_params(("arbitrary",), 48),
    )(x2d, omla, oc, osl, ow, mg, *weights)


def _moe_kernel(h2_ref, x1_ref, comb_ref, wgu_ref, wd_ref, o_ref, *, experts):
    step = pl.program_id(1)

    @pl.when(step == 0)
    def _():
        o_ref[...] = x1_ref[...]

    h = h2_ref[...]
    comb = comb_ref[...]
    lane = _lane_iota(comb.shape)
    out = None
    for i in range(experts):
        gu = _nn(h, wgu_ref[i])
        hid = jax.nn.silu(gu[:, :D_EXPERT]) * gu[:, D_EXPERT:]
        ce = jnp.sum(jnp.where(lane == step * experts + i, comb, 0.0), axis=1, keepdims=True)
        y = _nn((hid * ce).astype(BF16), wd_ref[i])
        out = y if out is None else out + y
    o_ref[...] += out


def _moe(h2, x1, comb, w, tm, experts):
    t, d = x1.shape
    tm = min(tm, t)
    assert N_EXPERTS % experts == 0
    tok = lambda n: pl.BlockSpec((tm, n), lambda i, e: (i, 0))
    return pl.pallas_call(
        functools.partial(_moe_kernel, experts=experts),
        out_shape=jax.ShapeDtypeStruct((t, d), F32),
        grid=(t // tm, N_EXPERTS // experts),
        in_specs=[tok(d), tok(d), tok(LANES),
                  pl.BlockSpec((experts, d, 2 * D_EXPERT), lambda i, e: (e, 0, 0)),
                  pl.BlockSpec((experts, D_EXPERT, d), lambda i, e: (e, 0, 0))],
        out_specs=tok(d),
        name="moe",
        compiler_params=_params(("arbitrary", "arbitrary"), 48),
    )(h2, x1, comb, w["w_gu"], w["w_down"])


def _mla_qabs_kernel(q_ref, g_ref, w_ref, o_ref):
    g = g_ref[...]
    for hd in range(MLA_HEADS):
        qg = (q_ref[:, LANES * hd:LANES * (hd + 1)] * g).astype(BF16)
        o_ref[:, 2 * LANES * hd:2 * LANES * (hd + 1)] = _nn(qg, w_ref[hd]).astype(BF16)


def _mla_qabs(qmf, w):
    t = qmf.shape[0]
    return pl.pallas_call(
        _mla_qabs_kernel,
        out_shape=jax.ShapeDtypeStruct((t, MLA_HEADS * 2 * LANES), BF16),
        grid=(1,),
        in_specs=[pl.BlockSpec(qmf.shape, lambda i: (0, 0)),
                  pl.BlockSpec((1, LANES), lambda i: (0, 0)),
                  pl.BlockSpec(w["w_abs"].shape, lambda i: (0, 0, 0))],
        out_specs=pl.BlockSpec((t, MLA_HEADS * 2 * LANES), lambda i: (0, 0)),
        name="mla_q_absorb",
        compiler_params=_params(("arbitrary",), 32),
    )(qmf, w["g_mla_k"], w["w_abs"])


def _page_copies(pt_ref, cache_ref, buf, sem, seq, first_page, slot, pages, feat0, feats, start):
    def one(p, carry):
        page = pt_ref[seq, first_page + p]
        row0 = pl.multiple_of((slot * pages + p) * feats, 8)
        cp = pltpu.make_async_copy(cache_ref.at[page, pl.ds(feat0, feats), :],
                                   buf.at[pl.ds(row0, feats), :], sem.at[slot])
        if start:
            cp.start()
        else:
            cp.wait()
        return carry
    lax.fori_loop(0, pages, one, 0)


def _paged_prefetch(pt_ref, cache_ref, buf, sem, pages, nchunk, feat0, feats):
    s = pl.program_id(0)
    c = pl.program_id(1)
    step = s * nchunk + c
    slot = step % 2
    copies = functools.partial(_page_copies, pt_ref, cache_ref, buf, sem, pages=pages, feat0=feat0, feats=feats)

    @pl.when(step == 0)
    def _():
        copies(seq=s, first_page=c * pages, slot=slot, start=True)

    @pl.when(step + 1 < pl.num_programs(0) * nchunk)
    def _():
        nxt = step + 1
        copies(seq=nxt // nchunk, first_page=(nxt % nchunk) * pages, slot=1 - slot, start=True)

    copies(seq=s, first_page=c * pages, slot=slot, start=False)
    return slot


def _key_tile(buf, slot, pages, feats, first, count, f0, nf):
    parts = []
    for j in range(count):
        row0 = pl.multiple_of((slot * pages + first + j) * feats + f0, 8)
        parts.append(buf[pl.ds(row0, nf), :])
    return jnp.concatenate(parts, axis=1) if count > 1 else parts[0]


def _mla_decode_kernel(pt_ref, cache_ref, qabs_ref, wkt_ref, newrows_ref, newmask_ref, o_ref,
                       buf, sem, m_sc, l_sc, acc_sc, *, pages, nchunk, ppsub, dec_seq):
    c = pl.program_id(1)
    slot = _paged_prefetch(pt_ref, cache_ref, buf, sem, pages, nchunk, 0, MLA_ROW)
    nk = wkt_ref.shape[0]

    @pl.when(c == 0)
    def _():
        m_sc[...] = jnp.full(m_sc.shape, NEG, F32)
        l_sc[...] = jnp.zeros(l_sc.shape, F32)
        acc_sc[...] = jnp.zeros(acc_sc.shape, F32)

    lhs = jnp.concatenate([wkt_ref[...], qabs_ref[0][:, :MLA_ROW]], axis=0)

    def attend(rows_t, mask):
        rb = rows_t.astype(BF16)
        big = _nn(lhs, rb)
        sq = big[:nk] * big[:nk]
        npe = MLA_HEADS * MLA_NOPE
        ss = jnp.sum(sq[:npe].reshape(MLA_HEADS, MLA_NOPE, sq.shape[1]), axis=1)
        ss = ss + jnp.sum(sq[npe:], axis=0, keepdims=True)
        rinv = lax.rsqrt(ss / MLA_QK + EPS)
        st = big[nk:]
        sc = jnp.concatenate([st[MLA_HEADS * tk:MLA_HEADS * (tk + 1)] * rinv for tk in range(dec_seq)], axis=0)
        if mask is not None:
            sc = sc + mask
        _online_update(sc, rb[:KV_LORA], m_sc, l_sc, acc_sc, 0, v_transposed=True)

    def sbody(i, carry):
        attend(_key_tile(buf, slot, pages, MLA_ROW, i * ppsub, ppsub, 0, MLA_ROW), None)
        return carry
    lax.fori_loop(0, pages // ppsub, sbody, 0)

    @pl.when(c == nchunk - 1)
    def _():
        attend(newrows_ref[0], newmask_ref[...])
        o_ref[0] = acc_sc[0] / l_sc[0]


def _mla_decode(page_table, cache_t, qabs, newrows_t, newmask, w, pages_per_step, ppsub, dec_seq):
    nseq, npages = page_table.shape
    pages = min(pages_per_step, npages)
    ppsub = min(ppsub, pages)
    assert npages % pages == 0 and pages % ppsub == 0
    nchunk = npages // pages
    rows = MLA_HEADS * dec_seq
    seqmap = lambda s, c, pt: (s, 0, 0)
    return pl.pallas_call(
        functools.partial(_mla_decode_kernel, pages=pages, nchunk=nchunk, ppsub=ppsub, dec_seq=dec_seq),
        out_shape=jax.ShapeDtypeStruct((nseq, rows, LANES), F32),
        grid_spec=pltpu.PrefetchScalarGridSpec(
            num_scalar_prefetch=1,
            grid=(nseq, nchunk),
            in_specs=[pl.BlockSpec(memory_space=pl.ANY),
                      pl.BlockSpec((1, rows, 2 * LANES), seqmap),
                      pl.BlockSpec(w["w_kt_aug"].shape, lambda s, c, pt: (0, 0)),
                      pl.BlockSpec((1, MLA_ROW, LANES), seqmap),
                      pl.BlockSpec(newmask.shape, lambda s, c, pt: (0, 0))],
            out_specs=pl.BlockSpec((1, rows, LANES), seqmap),
            scratch_shapes=[pltpu.VMEM((2 * pages * MLA_ROW, LANES), F32),
                            pltpu.SemaphoreType.DMA((2,)),
                            pltpu.VMEM((1, rows, LANES), F32),
                            pltpu.VMEM((1, rows, LANES), F32),
                            pltpu.VMEM((1, rows, LANES), F32)]),
        name="mla_decode",
        compiler_params=_params(("arbitrary", "arbitrary"), 48),
    )(page_table, cache_t, qabs, w["w_kt_aug"], newrows_t, newmask)


def _mla_vout_kernel(o_ref, w_ref, out_ref):
    for jp in range(MLA_HEADS // 2):
        pair = o_ref[:, 2 * LANES * jp:2 * LANES * (jp + 1)].astype(BF16)
        out_ref[:, LANES * jp:LANES * (jp + 1)] = _nn(pair, w_ref[jp]).astype(BF16)


def _mla_vout(olat, w):
    t = olat.shape[0]
    return pl.pallas_call(
        _mla_vout_kernel,
        out_shape=jax.ShapeDtypeStruct((t, MLA_HEADS * MLA_VDIM), BF16),
        grid=(1,),
        in_specs=[pl.BlockSpec(olat.shape, lambda i: (0, 0)),
                  pl.BlockSpec(w["w_v_pair"].shape, lambda i: (0, 0, 0))],
        out_specs=pl.BlockSpec((t, MLA_HEADS * MLA_VDIM), lambda i: (0, 0)),
        name="mla_v_up",
        compiler_params=_params(("arbitrary",), 32),
    )(olat, w["w_v_pair"])


def _compress_t_kernel(pt_ref, cache_ref, wk_ref, wv_ref, pe_ref, b1_ref, w2k_ref, w2v_ref, gk_ref,
                       kc_ref, vc_ref, buf, sem, acc, *, pages, nchunk, dchunk):
    feats = 2 * LANES
    s = pl.program_id(0)
    c = pl.program_id(1)
    step = s * nchunk + c
    slot = step % 2

    def copies(seq, chunk, slot_, start):
        def one(p, carry):
            page = pt_ref[seq, chunk * pages + p]
            cp = pltpu.make_async_copy(cache_ref.at[page, pl.ds(0, feats), :], buf.at[slot_, :, p, :], sem.at[slot_])
            if start:
                cp.start()
            else:
                cp.wait()
            return carry
        lax.fori_loop(0, pages, one, 0)

    @pl.when(step == 0)
    def _():
        copies(s, c, slot, True)

    @pl.when(step + 1 < pl.num_programs(0) * nchunk)
    def _():
        nxt = step + 1
        copies(nxt // nchunk, nxt % nchunk, 1 - slot, True)

    copies(s, c, slot, False)

    outs = []
    for cc, (w1_ref, w2_ref) in enumerate(((wk_ref, w2k_ref), (wv_ref, w2v_ref))):
        acc[...] = jnp.zeros(acc.shape, F32)
        for d0 in range(0, NSA_DH, dchunk):
            cols = [jnp.concatenate([buf[slot, cc * LANES + g * NSA_DH + d] for g in range(NSA_GROUPS)], axis=0)
                    for d in range(d0, d0 + dchunk)]
            ksl = slice(d0 * LANES, (d0 + dchunk) * LANES)
            lhs = jnp.concatenate(cols, axis=1) + pe_ref[cc, :, ksl]
            acc[...] += _nn(lhs.astype(BF16), w1_ref[ksl, :])
        hid = jax.nn.gelu(acc[...] + b1_ref[cc])
        outs.append(_nn(hid.astype(BF16), w2_ref[...]))
    kc_ref[0] = _pair_rms(outs[0], gk_ref[...]).reshape(NSA_GROUPS, pages, LANES).astype(BF16)
    vc_ref[0] = outs[1].reshape(NSA_GROUPS, pages, LANES).astype(BF16)


def _compress_t(page_table, cache_t, w, pages_per_step):
    nseq, npages = page_table.shape
    pages = min(pages_per_step, npages)
    assert npages % pages == 0
    nchunk = npages // pages
    const = lambda a: pl.BlockSpec(a.shape, lambda s, c, pt: (0,) * a.ndim)
    consts = [w["cmp_w1k_t"], w["cmp_w1v_t"], w["cmp_pe_t"], w["cmp_b1_t"], w["cmp_w2k_t"], w["cmp_w2v_t"],
              w["g_k_cmp"]]
    out_spec = pl.BlockSpec((1, NSA_GROUPS, pages, LANES), lambda s, c, pt: (s, 0, c, 0))
    kc, vc = pl.pallas_call(
        functools.partial(_compress_t_kernel, pages=pages, nchunk=nchunk, dchunk=8),
        out_shape=[jax.ShapeDtypeStruct((nseq, NSA_GROUPS, npages, LANES), BF16)] * 2,
        grid_spec=pltpu.PrefetchScalarGridSpec(
            num_scalar_prefetch=1,
            grid=(nseq, nchunk),
            in_specs=[pl.BlockSpec(memory_space=pl.ANY)] + [const(a) for a in consts],
            out_specs=[out_spec, out_spec],
            scratch_shapes=[pltpu.VMEM((2, 2 * LANES, pages, LANES), F32),
                            pltpu.SemaphoreType.DMA((2,)),
                            pltpu.VMEM((NSA_GROUPS * pages, 2 * CMP_HID), F32)]),
        name="compress_sample",
        compiler_params=_params(("arbitrary", "arbitrary"), 56),
    )(page_table, cache_t, *consts)

    def pair_pack(a):
        a = a.reshape(nseq, NSA_GROUPS, 2 * npages, NSA_DH)
        return jnp.transpose(a, (0, 2, 1, 3)).reshape(nseq, 2 * npages, LANES)
    return pair_pack(kc), pair_pack(vc)


def _group_rinv(kt, dec_rows):
    ksq = kt * kt
    r0 = lax.rsqrt(jnp.sum(ksq[:NSA_DH], axis=0, keepdims=True) / NSA_DH + EPS)
    r1 = lax.rsqrt(jnp.sum(ksq[NSA_DH:], axis=0, keepdims=True) / NSA_DH + EPS)
    first = lax.broadcasted_iota(jnp.int32, (dec_rows, 1), 0) < dec_rows // NSA_GROUPS
    return jnp.where(first, r0, r1)


def _scale_rows(st, rinv, dec_rows):
    return jnp.concatenate([st[dec_rows * r:dec_rows * (r + 1)] * rinv for r in range(NSA_HPG)], axis=0)


def _cmp_decode_kernel(q_ref, kc_ref, vc_ref, bias_ref, cand_ref, gates_ref, o_ref, selb_ref, *, dec_rows):
    s = _nt(q_ref[0], kc_ref[0]) + bias_ref[...]
    vis = bias_ref[...] > 0.5 * NEG
    m = jnp.max(s, axis=1, keepdims=True)
    p = jnp.exp(s - m) * jnp.where(vis, 1.0, 0.0)
    p = p / jnp.maximum(jnp.sum(p, axis=1, keepdims=True), 1e-30)
    o_ref[0] = _nn(p.astype(BF16), vc_ref[0]) * gates_ref[0][:, 0:1]
    imp = p[0:dec_rows]
    for r in range(1, NSA_HPG):
        imp = imp + p[dec_rows * r:dec_rows * (r + 1)]
    sel = _topk_select(jnp.where(cand_ref[...] > 0.5, imp, -1.0), N_SEL - 1)
    selb_ref[0] = jnp.where(sel > 0.0, 0.0, NEG).astype(BF16)


def _cmp_decode(q32, kc, vc, bias, cand, gates32):
    nseq, rows, _ = q32.shape
    nbs = kc.shape[1]
    dec_rows = rows // NSA_HPG
    seqmap = lambda s: (s, 0, 0)
    return pl.pallas_call(
        functools.partial(_cmp_decode_kernel, dec_rows=dec_rows),
        out_shape=[jax.ShapeDtypeStruct((nseq, rows, LANES), F32),
                   jax.ShapeDtypeStruct((nseq, dec_rows, nbs), BF16)],
        grid=(nseq,),
        in_specs=[pl.BlockSpec((1, rows, LANES), seqmap),
                  pl.BlockSpec((1, nbs, LANES), seqmap),
                  pl.BlockSpec((1, nbs, LANES), seqmap),
                  pl.BlockSpec(bias.shape, lambda s: (0, 0)),
                  pl.BlockSpec(cand.shape, lambda s: (0, 0)),
                  pl.BlockSpec((1, rows, LANES), seqmap)],
        out_specs=[pl.BlockSpec((1, rows, LANES), seqmap), pl.BlockSpec((1, dec_rows, nbs), seqmap)],
        name="cmp_decode",
        compiler_params=_params(("arbitrary",), 32),
    )(q32, kc, vc, bias, cand, gates32)


def _slc_decode_kernel(pt_ref, cache_ref, q_ref, mrows_ref, eb_ref, new_ref, newbias_ref,
                       lastbias_ref, gates_ref, o_ref, buf, sem, m_sc, l_sc, acc_sc, mask_sc,
                       *, pages, nchunk, ppsub, dec_rows):
    c = pl.program_id(1)
    feats = 2 * LANES
    slot = _paged_prefetch(pt_ref, cache_ref, buf, sem, pages, nchunk, 2 * LANES, feats)
    nsub = pages // ppsub

    @pl.when(c == 0)
    def _():
        m_sc[...] = jnp.full(m_sc.shape, NEG, F32)
        l_sc[...] = jnp.zeros(l_sc.shape, F32)
        acc_sc[...] = jnp.zeros(acc_sc.shape, F32)
        mask_sc[...] = _nn(mrows_ref[0], eb_ref[...])

    q = q_ref[0]

    def attend(kt, vt, extra):
        st = _nn(q, kt.astype(BF16))
        sc = _scale_rows(st, _group_rinv(kt, dec_rows), dec_rows) + extra
        _online_update(sc, vt.astype(BF16), m_sc, l_sc, acc_sc, 0, v_transposed=True)

    def sbody(i, carry):
        kt = _key_tile(buf, slot, pages, feats, i * ppsub, ppsub, 0, LANES)
        vt = _key_tile(buf, slot, pages, feats, i * ppsub, ppsub, LANES, LANES)
        gsub = c * nsub + i
        mk = mask_sc[pl.ds(pl.multiple_of(gsub * dec_rows, dec_rows), dec_rows), :]
        is_last = jnp.where(gsub == nchunk * nsub - 1, 1.0, 0.0)
        attend(kt, vt, jnp.concatenate([mk] * NSA_HPG, axis=0) + lastbias_ref[...] * is_last)
        return carry
    lax.fori_loop(0, nsub, sbody, 0)

    @pl.when(c == nchunk - 1)
    def _():
        new = new_ref[0]
        attend(new[:LANES], new[LANES:], newbias_ref[...])
        o_ref[0] = acc_sc[0] / l_sc[0] * gates_ref[0][:, 1:2]


def _slc_decode(page_table, cache_t, q32, mrows, new_t, newbias, lastbias, gates32, w, pages_per_step, ppsub):
    nseq, npages = page_table.shape
    pages = min(pages_per_step, npages)
    assert npages % pages == 0 and pages % ppsub == 0
    nchunk = npages // pages
    rows = q32.shape[1]
    dec_rows = rows // NSA_HPG
    seqmap = lambda s, c, pt: (s, 0, 0)
    const = lambda a: pl.BlockSpec(a.shape, lambda s, c, pt: (0,) * a.ndim)
    return pl.pallas_call(
        functools.partial(_slc_decode_kernel, pages=pages, nchunk=nchunk, ppsub=ppsub, dec_rows=dec_rows),
        out_shape=jax.ShapeDtypeStruct((nseq, rows, LANES), F32),
        grid_spec=pltpu.PrefetchScalarGridSpec(
            num_scalar_prefetch=1,
            grid=(nseq, nchunk),
            in_specs=[pl.BlockSpec(memory_space=pl.ANY),
                      pl.BlockSpec((1, rows, LANES), seqmap),
                      pl.BlockSpec((1,) + mrows.shape[1:], seqmap),
                      const(w["blk_expand"]),
                      pl.BlockSpec((1, 2 * LANES, LANES), seqmap),
                      const(newbias), const(lastbias),
                      pl.BlockSpec((1, rows, LANES), seqmap)],
            out_specs=pl.BlockSpec((1, rows, LANES), seqmap),
            scratch_shapes=[pltpu.VMEM((2 * pages * 2 * LANES, LANES), F32),
                            pltpu.SemaphoreType.DMA((2,)),
                            pltpu.VMEM((1, rows, LANES), F32),
                            pltpu.VMEM((1, rows, LANES), F32),
                            pltpu.VMEM((1, rows, LANES), F32),
                            pltpu.VMEM((mrows.shape[1], ppsub * LANES), F32)]),
        name="slc_decode",
        compiler_params=_params(("arbitrary", "arbitrary"), 48),
    )(page_table, cache_t, q32, mrows, w["blk_expand"], new_t, newbias, lastbias, gates32)


def _win_decode_kernel(q_ref, win_ref, new_ref, bias_ref, gates_ref, o_ref, *, dec_rows):
    win = win_ref[0]
    new = new_ref[0]
    kt = jnp.concatenate([win[:LANES], new[:LANES]], axis=1)
    vt = jnp.concatenate([win[LANES:], new[LANES:]], axis=1)
    st = _nn(q_ref[0], kt.astype(BF16))
    sc = _scale_rows(st, _group_rinv(kt, dec_rows), dec_rows) + bias_ref[...]
    m = jnp.max(sc, axis=1, keepdims=True)
    p = jnp.exp(sc - m)
    inv = gates_ref[0][:, 2:3] / jnp.sum(p, axis=1, keepdims=True)
    o_ref[0] = _nt(p.astype(BF16), vt.astype(BF16)) * inv


def _win_decode(q32, win_t, new_t, bias, gates32):
    nseq, rows, _ = q32.shape
    wbuf = win_t.shape[2]
    seqmap = lambda s: (s, 0, 0)
    return pl.pallas_call(
        functools.partial(_win_decode_kernel, dec_rows=rows // NSA_HPG),
        out_shape=jax.ShapeDtypeStruct((nseq, rows, LANES), F32),
        grid=(nseq,),
        in_specs=[pl.BlockSpec((1, rows, LANES), seqmap),
                  pl.BlockSpec((1, 2 * LANES, wbuf), seqmap),
                  pl.BlockSpec((1, 2 * LANES, LANES), seqmap),
                  pl.BlockSpec(bias.shape, lambda s: (0, 0)),
                  pl.BlockSpec((1, rows, LANES), seqmap)],
        out_specs=pl.BlockSpec((1, rows, LANES), seqmap),
        name="win_decode",
        compiler_params=_params(("arbitrary",), 32),
    )(q32, win_t, new_t, bias, gates32)


def _prep_weights(g_attn, w_in, g_q_lat, w_q_b, g_kv_lat, w_ukv, g_mla_q, g_mla_k, w_up_mla,
                  g_nsa_q, g_k_cmp, g_k_slc, g_k_win,
                  pe_cmp_k, w_cmp_k1, b_cmp_k1, w_cmp_k2, pe_cmp_v, w_cmp_v1, b_cmp_v1, w_cmp_v2,
                  w_up_nsa, w_out, g_ffn, w_router_group, b_router_group, w_router_expert, b_router_expert,
                  w_e_gate, w_e_up, w_e_down, *, blocks_per_tile):
    d = w_in.shape[0]
    half_rope = MLA_ROPE // 2
    z = lambda *shape: jnp.zeros(shape, F32)
    row = lambda v: v.reshape(1, -1).astype(F32)
    pad_to = lambda v, n: jnp.concatenate([v, z(v.shape[0], n - v.shape[1])], axis=1)

    def rot(cols):
        return jnp.concatenate([-cols[..., half_rope:], cols[..., :half_rope]], axis=-1)

    offs = np.cumsum([Q_LORA, KV_LORA, MLA_ROPE, NSA_HEADS * NSA_DH, 6 * NSA_GROUPS * NSA_DH, 3 * NSA_HEADS])
    c_q, c_kv, kpe, nq, nkv, ngate, mgate = jnp.split(w_in, offs.tolist(), axis=1)
    seg_a = jnp.concatenate([z(d, MLA_NOPE), kpe, z(d, LANES - MLA_QK)], axis=1)
    seg_b = jnp.concatenate([z(d, MLA_NOPE), rot(kpe), z(d, LANES - MLA_QK)], axis=1)
    nq_h = nq.reshape(d, NSA_HEADS, NSA_DH)
    nq_pp = jnp.concatenate([jnp.concatenate([nq_h[:, j], nq_h[:, j + NSA_HPG]], axis=1)
                             for j in range(NSA_HPG)], axis=1)
    w_in_p = jnp.concatenate([c_q, c_kv, seg_a, seg_b, nq_pp, nkv, pad_to(ngate, LANES), mgate], axis=1)

    wq = w_q_b.reshape(Q_LORA, MLA_HEADS, MLA_QK)
    zq = z(Q_LORA, MLA_HEADS, LANES - MLA_QK)
    wq_a = jnp.concatenate([wq, zq], axis=2).reshape(Q_LORA, MLA_HEADS * LANES)
    wq_b = jnp.concatenate([z(Q_LORA, MLA_HEADS, MLA_NOPE), rot(wq[..., MLA_NOPE:]), zq],
                           axis=2).reshape(Q_LORA, MLA_HEADS * LANES)
    wkv = w_ukv.reshape(KV_LORA, MLA_HEADS, MLA_NOPE + MLA_VDIM)
    wk = wkv[..., :MLA_NOPE]
    wv = wkv[..., MLA_NOPE:]
    wk_p = jnp.concatenate([wk, z(KV_LORA, MLA_HEADS, LANES - MLA_NOPE)], axis=2).reshape(KV_LORA, MLA_HEADS * LANES)
    g96 = lambda g: jnp.concatenate([g, z(LANES - MLA_QK)]).reshape(1, LANES)
    pair = lambda g: jnp.concatenate([g, g]).reshape(1, LANES).astype(F32)

    eye_pe = jnp.eye(MLA_ROPE, dtype=F32)
    w_abs = []
    for hd in range(MLA_HEADS):
        top = jnp.concatenate([wk[:, hd, :].T, z(MLA_NOPE, LANES)], axis=1)
        mid = jnp.concatenate([z(MLA_ROPE, KV_LORA), eye_pe, z(MLA_ROPE, LANES - MLA_ROPE)], axis=1)
        w_abs.append(jnp.concatenate([top, mid, z(LANES - MLA_QK, 2 * LANES)], axis=0))
    w_abs = jnp.stack(w_abs)
    wkt = jnp.concatenate([wk.reshape(KV_LORA, MLA_HEADS * MLA_NOPE).T, z(MLA_HEADS * MLA_NOPE, MLA_ROPE)], axis=1)
    w_kt_aug = jnp.concatenate([wkt, jnp.concatenate([z(MLA_ROPE, KV_LORA), eye_pe], axis=1)], axis=0)
    w_v_pair = []
    for jp in range(MLA_HEADS // 2):
        a = jnp.concatenate([wv[:, 2 * jp, :], z(KV_LORA, MLA_VDIM)], axis=1)
        b = jnp.concatenate([z(KV_LORA, MLA_VDIM), wv[:, 2 * jp + 1, :]], axis=1)
        w_v_pair.append(jnp.concatenate([a, b], axis=0))
    w_v_pair = jnp.stack(w_v_pair)

    def cmp_t(w1):
        return jnp.transpose(w1.reshape(CMP_BLOCK, NSA_DH, CMP_HID), (0, 2, 1))
    wk1t, wv1t = cmp_t(w_cmp_k1), cmp_t(w_cmp_v1)
    zz = z(CMP_BLOCK, CMP_HID, NSA_DH)
    cmp_wt = jnp.concatenate([
        jnp.concatenate([wk1t, zz, zz, zz], axis=2),
        jnp.concatenate([zz, wk1t, zz, zz], axis=2),
        jnp.concatenate([zz, zz, wv1t, zz], axis=2),
        jnp.concatenate([zz, zz, zz, wv1t], axis=2)], axis=1)
    cmp_pe = jnp.concatenate([pe_cmp_k, pe_cmp_k, pe_cmp_v, pe_cmp_v], axis=1).reshape(CMP_BLOCK, 1, 2 * LANES)
    cmp_b1 = jnp.concatenate([b_cmp_k1, b_cmp_k1, b_cmp_v1, b_cmp_v1]).reshape(4 * CMP_HID, 1)

    def w2_bd(w2):
        zt = z(NSA_DH, CMP_HID)
        return jnp.concatenate([jnp.concatenate([w2.T, zt], axis=1), jnp.concatenate([zt, w2.T], axis=1)], axis=0)

    def w1_pages(w1):
        wd = jnp.transpose(w1.reshape(CMP_BLOCK, NSA_DH, CMP_HID), (1, 0, 2))
        zero = jnp.zeros_like(wd)
        both = jnp.stack([jnp.concatenate([wd, zero], axis=2), jnp.concatenate([zero, wd], axis=2)], axis=1)
        return both.reshape(NSA_DH * 2 * CMP_BLOCK, 2 * CMP_HID)

    def pe_pages(pe):
        return jnp.broadcast_to(pe.T[:, None, :], (NSA_DH, 2, CMP_BLOCK)).reshape(1, NSA_DH * 2 * CMP_BLOCK)

    def w2_pages(w2):
        zt = z(CMP_HID, NSA_DH)
        return jnp.concatenate([jnp.concatenate([w2, zt], axis=1), jnp.concatenate([zt, w2], axis=1)], axis=0)

    perm = np.array([HALF * (jp + NSA_HPG * half) + dd for jp in range(NSA_HPG)
                     for half in range(NSA_GROUPS) for dd in range(NSA_DH)])
    w_router = pad_to(jnp.concatenate([w_router_expert, w_router_group], axis=1), LANES)
    b_router = pad_to(jnp.concatenate([b_router_expert, b_router_group]).reshape(1, -1), LANES)

    blk_expand = np.zeros((LANES, blocks_per_tile * CMP_BLOCK), np.float32)
    for i in range(blocks_per_tile):
        blk_expand[i, CMP_BLOCK * i:CMP_BLOCK * (i + 1)] = 1.0

    return dict(
        g_attn=row(g_attn), w_in=w_in_p.astype(BF16), g_q_lat=row(g_q_lat),
        w_q=jnp.concatenate([wq_a, wq_b], axis=1).astype(BF16), g_kv_lat=row(g_kv_lat),
        w_k=wk_p.astype(BF16), w_v=wv.reshape(KV_LORA, MLA_HEADS * MLA_VDIM).astype(BF16),
        g_mla_q=g96(g_mla_q), g_mla_k=g96(g_mla_k), g_nsa_q=pair(g_nsa_q), g_k_slc=pair(g_k_slc),
        g_k_win=pair(g_k_win), g_k_cmp=pair(g_k_cmp),
        g_k_cmp_col=jnp.concatenate([g_k_cmp, g_k_cmp]).reshape(LANES, 1).astype(F32),
        w_abs=w_abs.astype(BF16), w_kt_aug=w_kt_aug.astype(BF16), w_v_pair=w_v_pair.astype(BF16),
        cmp_wt=cmp_wt.astype(BF16), cmp_pe=cmp_pe.astype(F32), cmp_b1=cmp_b1.astype(F32),
        cmp_w2k=w2_bd(w_cmp_k2).astype(BF16), cmp_w2v=w2_bd(w_cmp_v2).astype(BF16),
        cmp_w1k_t=w1_pages(w_cmp_k1).astype(BF16), cmp_w1v_t=w1_pages(w_cmp_v1).astype(BF16),
        cmp_pe_t=jnp.stack([pe_pages(pe_cmp_k), pe_pages(pe_cmp_v)]).astype(F32),
        cmp_b1_t=jnp.stack([jnp.concatenate([b_cmp_k1, b_cmp_k1]).reshape(1, -1),
                            jnp.concatenate([b_cmp_v1, b_cmp_v1]).reshape(1, -1)]).astype(F32),
        cmp_w2k_t=w2_pages(w_cmp_k2).astype(BF16), cmp_w2v_t=w2_pages(w_cmp_v2).astype(BF16),
        w_up_mla=w_up_mla.astype(BF16), w_up_nsa=w_up_nsa[perm].astype(BF16), w_out=w_out.astype(BF16),
        g_ffn=row(g_ffn), w_router=w_router.astype(BF16), b_router=b_router.astype(F32),
        w_gu=jnp.concatenate([w_e_gate, w_e_up], axis=2).astype(BF16), w_down=w_e_down.astype(BF16),
        blk_expand=jnp.asarray(blk_expand, BF16),
    )


def _rope_table(pos):
    half = MLA_ROPE // 2
    inv = ROPE_BASE ** (-jnp.arange(half, dtype=F32) / half)
    ang = pos.astype(F32)[:, None] * inv
    cos, sin = jnp.cos(ang), jnp.sin(ang)
    t = pos.shape[0]
    one, zero = jnp.ones((t, MLA_NOPE), F32), jnp.zeros((t, MLA_NOPE), F32)
    zpad = jnp.zeros((t, LANES - MLA_QK), F32)
    return jnp.concatenate([one, cos, cos, zpad, zero, sin, sin, zpad], axis=1)


TM_PROJ = 256
TQ_MLA = 1024
TQ_CMP = 512
TQ_SLC = 1024
TQ_WIN = 256
TM_MERGE = 256
TM_MOE = 1024
MOE_EXPERTS_PER_STEP = 2
CMP_PAGES_PROMPT = 64
CMP_PAGES = 64
DEC_PAGES = 32
DEC_PPSUB = 4


def kernel(x_prompt, x_sample, cache_mla, cache_nsa_kv, state_nsa_win, page_table, g_attn, w_in, g_q_lat, w_q_b,
           g_kv_lat, w_ukv, g_mla_q, g_mla_k, w_up_mla, g_nsa_q, g_k_cmp, g_k_slc, g_k_win, pe_cmp_k, w_cmp_k1,
           b_cmp_k1, w_cmp_k2, pe_cmp_v, w_cmp_v1, b_cmp_v1, w_cmp_v2, rel_bias, w_up_nsa, w_out, g_ffn,
           w_router_group, b_router_group, w_router_expert, b_router_expert, w_e_gate, w_e_up, w_e_down):
    assert w_in.shape[0] == 1, "single-layer step"
    b, s, d = x_prompt.shape
    nseq, dec_seq, _ = x_sample.shape
    n_pool, page_size = cache_mla.shape[1], cache_mla.shape[2]
    npages = page_table.shape[1]
    past = npages * page_size
    wbuf = state_nsa_win.shape[2]
    nbs = past // CMP_BLOCK
    dec_pages = min(DEC_PAGES, npages)
    ppsub = min(DEC_PPSUB, dec_pages)
    sub_keys = ppsub * page_size
    bps = sub_keys // CMP_BLOCK
    assert page_size == LANES == 2 * CMP_BLOCK and dec_seq < CMP_BLOCK - 1 and s % CMP_BLOCK == 0
    assert past % sub_keys == 0 and wbuf % LANES == 0

    w = _prep_weights(g_attn[0], w_in[0], g_q_lat[0], w_q_b[0], g_kv_lat[0], w_ukv[0], g_mla_q[0], g_mla_k[0],
                      w_up_mla[0], g_nsa_q[0], g_k_cmp[0], g_k_slc[0], g_k_win[0],
                      pe_cmp_k[0], w_cmp_k1[0], b_cmp_k1[0], w_cmp_k2[0], pe_cmp_v[0], w_cmp_v1[0], b_cmp_v1[0],
                      w_cmp_v2[0], w_up_nsa[0], w_out[0], g_ffn[0], w_router_group[0], b_router_group[0],
                      w_router_expert[0], b_router_expert[0], w_e_gate[0], w_e_up[0], w_e_down[0],
                      blocks_per_tile=bps)
    tq_win = min(TQ_WIN, s)
    bias = _bias_tables(rel_bias, tq_win, past, dec_seq, wbuf, nbs, sub_keys)

    tp = b * s
    pp = _project(x_prompt.reshape(tp, d), _rope_table(jnp.arange(s, dtype=jnp.int32)), w, TM_PROJ, False)
    r3 = lambda a: a.reshape(b, s, a.shape[-1])
    o_mla_p = _mla_prefill(r3(pp["qm"]), r3(pp["km"]), r3(pp["vm"]), TQ_MLA)

    ident = jnp.arange(tp // page_size, dtype=jnp.int32).reshape(b, s // page_size)
    kct_p, vct_p = _compress(ident, pp["nsa4"].reshape(tp // page_size, page_size, 4 * LANES), w, CMP_PAGES_PROMPT)
    tq_cmp = min(TQ_CMP, s)
    ridx = np.arange(tq_cmp) % CMP_BLOCK
    f = bias["f"]
    cols = jnp.concatenate([f[:, 1 + ridx].T, f[:, 1 + CMP_BLOCK + ridx].T,
                            jnp.broadcast_to(f[:, 0][None, :], (tq_cmp, NSA_HEADS)),
                            jnp.zeros((tq_cmp, LANES - 3 * NSA_HEADS), F32)], axis=1)
    gates_p = r3(pp["gates"])
    o_cmp_p, selb_p = _cmp_prefill(r3(pp["nq"]), kct_p, vct_p, cols, gates_p, TQ_CMP)
    o_slc_p = _slc_prefill(r3(pp["nq"]), selb_p, r3(pp["slck"]), r3(pp["slcv"]), bias["td"], bias["ts"],
                           gates_p, TQ_SLC)
    o_win_p = _win_prefill(r3(pp["nq"]), r3(pp["wink"]), r3(pp["winv"]), bias["win"], gates_p, tq_win)
    flat = lambda a: a.reshape(tp, a.shape[-1])
    x1_p, h2_p, comb_p = _merge(x_prompt.reshape(tp, d), flat(o_mla_p), flat(o_cmp_p), flat(o_slc_p),
                                flat(o_win_p), pp["mg"], w, TM_MERGE)
    y_p = _moe(h2_p, x1_p, comb_p, w, TM_MOE, MOE_EXPERTS_PER_STEP).reshape(b, s, d)

    cache_mla_t = jnp.transpose(cache_mla[0], (0, 2, 1))
    cache_nsa_t = jnp.transpose(cache_nsa_kv[0], (0, 2, 3, 4, 1)).reshape(n_pool, 4 * LANES, page_size)
    win_t = jnp.transpose(state_nsa_win[0], (0, 2, 3, 4, 1)).reshape(nseq, 2 * LANES, wbuf)

    ts_ = nseq * dec_seq
    pos_s = past + jnp.arange(dec_seq, dtype=jnp.int32)
    ps = _project(x_sample.reshape(ts_, d), jnp.tile(_rope_table(pos_s), (nseq, 1)), w, TM_PROJ, True)

    def new_cols(a):
        a = jnp.transpose(a.reshape(nseq, dec_seq, a.shape[-1]), (0, 2, 1))
        return jnp.concatenate([a, jnp.zeros((nseq, a.shape[1], LANES - dec_seq), F32)], axis=2)

    rows_h = MLA_HEADS * dec_seq
    qabs = _mla_qabs(ps["qmf"], w).reshape(nseq, rows_h, 2 * LANES)
    tok_r = np.repeat(np.arange(dec_seq), MLA_HEADS)[:, None]
    ncol = np.arange(LANES)[None, :]
    newmask = jnp.asarray(np.where((ncol <= tok_r) & (ncol < dec_seq), 0.0, NEG).astype(np.float32))
    olat = _mla_decode(page_table, cache_mla_t, qabs, new_cols(ps["rows"]), newmask, w, dec_pages, ppsub, dec_seq)
    o_mla_s = _mla_vout(olat.reshape(ts_, MLA_HEADS * LANES), w)

    rows_n = NSA_HEADS * dec_seq
    dec_rows = NSA_GROUPS * dec_seq
    halfmask = jnp.asarray(np.stack([np.arange(LANES) < HALF, np.arange(LANES) >= HALF]).astype(np.float32), BF16)

    def q_rows(nq_pp):
        x = nq_pp.reshape(nseq, dec_seq, NSA_HPG, LANES)
        x = jnp.transpose(x, (0, 2, 1, 3))[:, :, None, :, :] * halfmask[None, None, :, None, :]
        return x.reshape(nseq, rows_n, LANES)

    gsig = ps["gates"].reshape(nseq, dec_seq, LANES)[:, :, :3 * NSA_HEADS]
    gsig = gsig.reshape(nseq, dec_seq, 3, NSA_GROUPS, NSA_HPG)
    gates32 = jnp.transpose(gsig, (0, 4, 3, 1, 2)).reshape(nseq, rows_n, 3)
    gates32 = jnp.concatenate([gates32, jnp.zeros((nseq, rows_n, LANES - 3), F32)], axis=2)

    kc_s, vc_s = _compress_t(page_table, cache_nsa_t, w, CMP_PAGES)
    cand = np.repeat((np.arange(nbs)[None, :] < ((past + np.arange(dec_seq)) // CMP_BLOCK)[:, None]
                      ).astype(np.float32)[None], NSA_GROUPS, axis=0).reshape(dec_rows, nbs)
    o_cmp32, selb8 = _cmp_decode(q_rows(ps["nq"]), kc_s, vc_s, _sample_rows(bias["s_cmp"]),
                                 jnp.asarray(cand), gates32)
    nsub_total = past // sub_keys
    mrows = jnp.transpose(selb8.reshape(nseq, dec_rows, nsub_total, bps), (0, 2, 1, 3))
    mrows = jnp.concatenate([mrows.reshape(nseq, nsub_total * dec_rows, bps),
                             jnp.zeros((nseq, nsub_total * dec_rows, LANES - bps), BF16)], axis=2)
    o_slc32 = _slc_decode(page_table, cache_nsa_t, q_rows(ps["nq_slc"]), mrows, new_cols(ps["nsa4"][:, 2 * LANES:]),
                          _sample_rows(bias["s_new"]), _sample_rows(bias["s_last"]), gates32, w, dec_pages, ppsub)
    new_win_t = new_cols(ps["win2"])
    o_win32 = _win_decode(q_rows(ps["nq_win"]), win_t, new_win_t, _sample_rows(bias["s_win"]), gates32)

    def from_rows(o32):
        x = o32.reshape(nseq, NSA_HPG, NSA_GROUPS, dec_seq, NSA_GROUPS, HALF)
        x = jnp.stack([x[:, :, 0, :, 0, :], x[:, :, 1, :, 1, :]], axis=3)
        return jnp.transpose(x, (0, 2, 1, 3, 4)).reshape(ts_, 4 * LANES)

    x1_s, h2_s, comb_s = _merge(x_sample.reshape(ts_, d), o_mla_s, from_rows(o_cmp32), from_rows(o_slc32),
                                from_rows(o_win32), ps["mg"], w, TM_MERGE)
    y_s = _moe(h2_s, x1_s, comb_s, w, TM_MOE, MOE_EXPERTS_PER_STEP).reshape(nseq, dec_seq, d)

    win_keep = min(WINDOW, s)
    new_mla_prompt = pp["rows"].reshape(1, b, s, MLA_ROW)
    new_mla_sample = ps["rows"].reshape(1, nseq, dec_seq, MLA_ROW)
    new_nsa_prompt = pp["nsa4"].reshape(1, b, s, 4, NSA_GROUPS, NSA_DH)
    new_nsa_sample = ps["nsa4"].reshape(1, nseq, dec_seq, 4, NSA_GROUPS, NSA_DH)
    new_win_prompt = pp["win2"].reshape(b, s, 2, NSA_GROUPS, NSA_DH)[None, :, s - win_keep:]
    nw_t = jnp.concatenate([win_t[:, :, dec_seq:], new_win_t[:, :, :dec_seq]], axis=2)
    new_win_sample = jnp.transpose(nw_t.reshape(nseq, 2, NSA_GROUPS, NSA_DH, wbuf), (0, 4, 1, 2, 3))[None]
    return (y_p, y_s, new_mla_prompt, new_mla_sample, new_nsa_prompt, new_nsa_sample, new_win_prompt,
            new_win_sample)
```

```python
import functools
import math

import numpy as np
import jax
import jax.numpy as jnp
from jax import lax
from jax.experimental import pallas as pl
from jax.experimental.pallas import tpu as pltpu

F32 = jnp.float32
BF16 = jnp.bfloat16

MLA_HEADS = 8
MLA_NOPE = 64
MLA_ROPE = 32
MLA_QK = MLA_NOPE + MLA_ROPE
MLA_VDIM = 64
Q_LORA = 256
KV_LORA = 128
MLA_ROW = KV_LORA + MLA_ROPE
ROPE_BASE = 10000.0
NSA_HEADS = 8
NSA_GROUPS = 2
NSA_HPG = NSA_HEADS // NSA_GROUPS
NSA_DH = 64
CMP_BLOCK = 64
CMP_HID = 128
N_SEL = 16
WINDOW = 512
N_BUCKETS = 32
MAX_DISTANCE = 128
N_GROUPS = 4
EXPERTS_PER_GROUP = 8
N_EXPERTS = N_GROUPS * EXPERTS_PER_GROUP
D_EXPERT = 256
EPS = 1e-6
NEG = -1e30
MLA_SCALE = MLA_QK ** -0.5
NSA_SCALE = NSA_DH ** -0.5

LANES = 128
HALF = LANES // 2
SUB = 128
LOOKUP_CHUNK = 8192

O_CQ = 0
O_CKV = O_CQ + Q_LORA
O_PEA = O_CKV + KV_LORA
O_PEB = O_PEA + LANES
O_NQ = O_PEB + LANES
O_NKV = O_NQ + NSA_HEADS * NSA_DH
O_GATE = O_NKV + 6 * NSA_GROUPS * NSA_DH
O_MG = O_GATE + LANES

_NT_DIMS = (((1,), (1,)), ((), ()))


def _nt(a, b):
    return lax.dot_general(a, b, _NT_DIMS, preferred_element_type=F32)


def _nn(a, b):
    return jnp.dot(a, b, preferred_element_type=F32)


def _params(sem, vmem_mb):
    return pltpu.CompilerParams(dimension_semantics=sem, vmem_limit_bytes=vmem_mb * 1024 * 1024)


def _lane_iota(shape):
    return lax.broadcasted_iota(jnp.int32, shape, len(shape) - 1)


def _pair_rms(x, gpair):
    lo = _lane_iota(x.shape) < HALF
    sq = x * x
    s_lo = jnp.sum(jnp.where(lo, sq, 0.0), axis=-1, keepdims=True)
    s_hi = jnp.sum(jnp.where(lo, 0.0, sq), axis=-1, keepdims=True)
    rinv = jnp.where(lo, lax.rsqrt(s_lo / NSA_DH + EPS), lax.rsqrt(s_hi / NSA_DH + EPS))
    return x * rinv * gpair


def _online_update(s, v, m_ref, l_ref, acc_ref, idx, v_transposed=False):
    m_prev = m_ref[idx]
    m_new = jnp.maximum(m_prev, jnp.max(s, axis=1, keepdims=True))
    alpha = jnp.exp(m_prev - m_new)
    p = jnp.exp(s - m_new[:, :1])
    l_ref[idx] = alpha * l_ref[idx] + jnp.sum(p, axis=1, keepdims=True)
    m_ref[idx] = m_new
    pb = p.astype(BF16)
    pv = _nt(pb, v) if v_transposed else _nn(pb, v)
    acc_ref[idx] = acc_ref[idx] * alpha + pv


def _topk_select(scores, n_top):
    lane = _lane_iota(scores.shape).astype(F32)
    big = float(scores.shape[-1])

    def body(_, carry):
        sc, sel = carry
        m = jnp.max(sc, axis=1, keepdims=True)
        first = jnp.min(jnp.where(sc == m, lane, big), axis=1, keepdims=True)
        hit = lane == first
        sel = jnp.where(hit & (m >= 0.0), 1.0, sel)
        sc = jnp.where(hit, -2.0, sc)
        return sc, sel

    _, sel = lax.fori_loop(0, n_top, body, (scores, jnp.zeros_like(scores)))
    return sel


def _bias_lookup_kernel(tbl_ref, bucket_ref, neg_ref, out_ref):
    tbl = tbl_ref[...]
    b = bucket_ref[...]
    acc = jnp.zeros(out_ref.shape, F32)
    for k in range(N_BUCKETS):
        acc = jnp.where(b == k, tbl[:, k:k + 1], acc)
    out_ref[...] = acc - tbl[:, N_BUCKETS - 1:N_BUCKETS] + neg_ref[...]


def _bias_lookup(table_t, bucket, neg):
    n = bucket.shape[1]
    assert n % LOOKUP_CHUNK == 0
    return pl.pallas_call(
        _bias_lookup_kernel,
        out_shape=jax.ShapeDtypeStruct((NSA_HEADS, n), F32),
        grid=(n // LOOKUP_CHUNK,),
        in_specs=[pl.BlockSpec((NSA_HEADS, N_BUCKETS), lambda i: (0, 0)),
                  pl.BlockSpec((1, LOOKUP_CHUNK), lambda i: (0, i)),
                  pl.BlockSpec((1, LOOKUP_CHUNK), lambda i: (0, i))],
        out_specs=pl.BlockSpec((NSA_HEADS, LOOKUP_CHUNK), lambda i: (0, i)),
        name="bias_lookup",
        compiler_params=_params(("arbitrary",), 32),
    )(table_t, bucket, neg)


def _t5_bucket_np(dist):
    n = np.maximum(dist, 0)
    max_exact = N_BUCKETS // 2
    log_ratio = np.log(np.maximum(n, 1).astype(np.float32) / np.float32(max_exact)) / np.float32(
        math.log(MAX_DISTANCE / max_exact))
    large = max_exact + (log_ratio.astype(np.float32) * (N_BUCKETS - max_exact)).astype(np.int32)
    return np.where(n < max_exact, n, np.minimum(large, N_BUCKETS - 1)).astype(np.int32)


def _bias_tables(rel_bias, tq_win, past, dec_seq, wbuf, nbs, sub_keys):
    pieces = {}

    def add(name, dist, valid):
        dist = np.asarray(dist)
        pieces[name] = (dist.shape, _t5_bucket_np(dist).reshape(-1),
                        np.where(np.asarray(valid), 0.0, NEG).astype(np.float32).reshape(-1))

    d = np.arange(2 * SUB)
    add("f", d, np.ones_like(d, bool))
    i = np.arange(SUB)[:, None]
    j = np.arange(SUB)[None, :]
    add("td", i - j, i >= j)
    add("ts", SUB + i - j, np.ones((SUB, SUB), bool))
    nkb = WINDOW // tq_win + 1
    iw = np.arange(tq_win)[:, None]
    cw = np.arange(nkb * tq_win)[None, :]
    dw = WINDOW + iw - cw
    add("win", dw, (dw >= 0) & (dw < WINDOW))
    tok = np.arange(dec_seq)[:, None]
    pos = past + tok
    jc = np.arange(nbs)[None, :]
    dc = pos - (jc * CMP_BLOCK + CMP_BLOCK - 1)
    add("s_cmp", dc, dc >= 0)
    wpad = wbuf + LANES
    w = np.arange(wpad)[None, :]
    dsw = wbuf + tok - w
    add("s_win", dsw, (dsw >= 0) & (dsw < WINDOW) & (w < wbuf + dec_seq))
    n = np.arange(LANES)[None, :]
    add("s_new", tok - n, (n <= tok) & (n < dec_seq))
    kl = np.arange(sub_keys)[None, :]
    add("s_last", pos - (past - sub_keys + kl), np.ones((dec_seq, sub_keys), bool))

    total = sum(v[1].size for v in pieces.values())
    padded = -(-total // LOOKUP_CHUNK) * LOOKUP_CHUNK
    bucket = np.zeros((1, padded), np.int32)
    neg = np.zeros((1, padded), np.float32)
    off = 0
    spans = {}
    for name, (shape, b, ng) in pieces.items():
        bucket[0, off:off + b.size] = b
        neg[0, off:off + b.size] = ng
        spans[name] = (off, shape)
        off += b.size
    flat = _bias_lookup(rel_bias.T.astype(F32), jnp.asarray(bucket), jnp.asarray(neg))
    out = {}
    for name, (o, shape) in spans.items():
        out[name] = flat[:, o:o + int(np.prod(shape))].reshape((NSA_HEADS,) + tuple(shape))
    return out


def _sample_rows(tile):
    h, t, w = tile.shape
    x = tile.reshape(NSA_GROUPS, NSA_HPG, t, w)
    return jnp.transpose(x, (1, 0, 2, 3)).reshape(h * t, w)


def _proj_kernel(x_ref, cs_ref, gattn_ref, win_ref, gq_ref, wq_ref, gkv_ref, wk_ref, wv_ref,
                 gmq_ref, gmk_ref, gnq_ref, gslc_ref, gwin_ref,
                 rows_ref, qm_ref, km_ref, vm_ref, nq_ref, nsa4_ref, win2_ref,
                 slck_ref, slcv_ref, wink_ref, winv_ref, gates_ref, mg_ref, *extra, d_model, sample):
    x = x_ref[...]
    ms = jnp.mean(x * x, axis=-1, keepdims=True)
    h = (x * lax.rsqrt(ms + EPS) * gattn_ref[...]).astype(BF16)

    def seg(o, n):
        return _nn(h, win_ref[:, o:o + n])

    cs = cs_ref[...]
    cos_t = cs[:, :LANES]
    sin_t = cs[:, LANES:]

    ckv = seg(O_CKV, KV_LORA)
    lat = ckv * lax.rsqrt(jnp.mean(ckv * ckv, axis=-1, keepdims=True) + EPS) * gkv_ref[...]
    kpe = seg(O_PEA, LANES) * cos_t + seg(O_PEB, LANES) * sin_t
    rows_ref[:, 0:KV_LORA] = lat
    rows_ref[:, KV_LORA:MLA_ROW] = pltpu.roll(kpe, HALF, 1)[:, 0:MLA_ROPE]

    cq = seg(O_CQ, Q_LORA)
    cqn = (cq * lax.rsqrt(jnp.mean(cq * cq, axis=-1, keepdims=True) + EPS) * gq_ref[...]).astype(BF16)
    qab = _nn(cqn, wq_ref[...])
    lat_b = lat.astype(BF16)
    kall = _nn(lat_b, wk_ref[...])
    nh = MLA_HEADS * LANES
    for hd in range(MLA_HEADS):
        sl = slice(LANES * hd, LANES * (hd + 1))
        qh = qab[:, sl] * cos_t + qab[:, nh + LANES * hd:nh + LANES * (hd + 1)] * sin_t
        qn = qh * lax.rsqrt(jnp.sum(qh * qh, axis=-1, keepdims=True) / MLA_QK + EPS) * gmq_ref[...] * MLA_SCALE
        qm_ref[:, sl] = qn.astype(BF16)
        if sample:
            extra[0][:, sl] = qn
        kh = kall[:, sl] + kpe
        kn = kh * lax.rsqrt(jnp.sum(kh * kh, axis=-1, keepdims=True) / MLA_QK + EPS) * gmk_ref[...]
        km_ref[:, sl] = kn.astype(BF16)
    vm_ref[...] = _nn(lat_b, wv_ref[...]).astype(BF16)

    nq = seg(O_NQ, NSA_HEADS * NSA_DH)
    for j in range(NSA_HPG):
        sl = slice(LANES * j, LANES * (j + 1))
        qj = _pair_rms(nq[:, sl], gnq_ref[...]) * NSA_SCALE
        nq_ref[:, sl] = qj.astype(BF16)
        if sample:
            extra[1][:, sl] = (qj * gslc_ref[...]).astype(BF16)
            extra[2][:, sl] = (qj * gwin_ref[...]).astype(BF16)
    nkv = seg(O_NKV, 6 * LANES)
    nsa4_ref[...] = nkv[:, :4 * LANES]
    if not sample:
        extra[0][0] = nkv[:, :4 * LANES].T
    win2_ref[...] = nkv[:, 4 * LANES:]
    slck_ref[...] = _pair_rms(nkv[:, 2 * LANES:3 * LANES], gslc_ref[...]).astype(BF16)
    slcv_ref[...] = nkv[:, 3 * LANES:4 * LANES].astype(BF16)
    wink_ref[...] = _pair_rms(nkv[:, 4 * LANES:5 * LANES], gwin_ref[...]).astype(BF16)
    winv_ref[...] = nkv[:, 5 * LANES:].astype(BF16)
    gates_ref[...] = jax.nn.sigmoid(seg(O_GATE, LANES))
    mg_ref[...] = jax.nn.sigmoid(seg(O_MG, 2 * d_model))


def _project(x2d, cs, w, tm, sample, seq_len=None):
    t, d = x2d.shape
    tm = min(tm, t)
    assert t % tm == 0 and cs.shape[0] % tm == 0
    n_cs = cs.shape[0] // tm
    full = lambda a: pl.BlockSpec(a.shape, lambda i: (0,) * a.ndim)
    tok = lambda n: pl.BlockSpec((tm, n), lambda i: (i, 0))
    weights = [w["g_attn"], w["w_in"], w["g_q_lat"], w["w_q"], w["g_kv_lat"], w["w_k"], w["w_v"],
               w["g_mla_q"], w["g_mla_k"], w["g_nsa_q"], w["g_k_slc"], w["g_k_win"]]
    outs = [("rows", MLA_ROW, F32), ("qm", 8 * LANES, BF16), ("km", 8 * LANES, BF16), ("vm", 4 * LANES, BF16),
            ("nq", 4 * LANES, BF16), ("nsa4", 4 * LANES, F32), ("win2", 2 * LANES, F32),
            ("slck", LANES, BF16), ("slcv", LANES, BF16), ("wink", LANES, BF16), ("winv", LANES, BF16),
            ("gates", LANES, F32), ("mg", 2 * d, F32)]
    if sample:
        outs += [("qmf", 8 * LANES, F32), ("nq_slc", 4 * LANES, BF16), ("nq_win", 4 * LANES, BF16)]
    out_shape = [jax.ShapeDtypeStruct((t, n), dt) for _, n, dt in outs]
    out_specs = [tok(n) for _, n, _ in outs]
    names = [name for name, _, _ in outs]
    if not sample:
        nps = seq_len // tm
        names.append("nsa4_t")
        out_shape.append(jax.ShapeDtypeStruct((t // seq_len, 4 * LANES, seq_len), F32))
        out_specs.append(pl.BlockSpec((1, 4 * LANES, tm), lambda i: (i // nps, 0, i % nps)))
    res = pl.pallas_call(
        functools.partial(_proj_kernel, d_model=d, sample=sample),
        out_shape=out_shape,
        grid=(t // tm,),
        in_specs=[tok(d), pl.BlockSpec((tm, 2 * LANES), lambda i: (i % n_cs, 0))] + [full(a) for a in weights],
        out_specs=out_specs,
        name="project_sample" if sample else "project_prompt",
        compiler_params=_params(("arbitrary",), 56),
    )(x2d, cs, *weights)
    return dict(zip(names, res))


def _tri_tables(n):
    qi = np.concatenate([np.full(q + 1, q) for q in range(n)]).astype(np.int32)
    ki = np.concatenate([np.arange(q + 1) for q in range(n)]).astype(np.int32)
    return jnp.asarray(qi), jnp.asarray(ki)


def _mla_prefill_kernel(qt_ref, kt_ref, q_ref, k_ref, v_ref, o_ref, m_sc, l_sc, acc_sc, *, tq):
    qi = qt_ref[pl.program_id(2)]
    ki = kt_ref[pl.program_id(2)]

    @pl.when(ki == 0)
    def _():
        m_sc[...] = jnp.full(m_sc.shape, NEG, F32)
        l_sc[...] = jnp.zeros(l_sc.shape, F32)
        acc_sc[...] = jnp.zeros(acc_sc.shape, F32)

    def body(diag):
        v = v_ref[0]
        for hh in range(2):
            sl = slice(LANES * hh, LANES * (hh + 1))
            s = _nt(q_ref[0, :, sl], k_ref[0, :, sl])
            if diag:
                row = lax.broadcasted_iota(jnp.int32, s.shape, 0)
                col = lax.broadcasted_iota(jnp.int32, s.shape, 1)
                s = jnp.where(col <= row, s, NEG)
            _online_update(s, v, m_sc, l_sc, acc_sc, hh)

    @pl.when(ki < qi)
    def _():
        body(False)

    @pl.when(ki == qi)
    def _():
        body(True)
        lo = _lane_iota((tq, LANES)) < HALF
        o_ref[0] = jnp.where(lo, acc_sc[0] / l_sc[0], acc_sc[1] / l_sc[1]).astype(BF16)


def _mla_prefill(qm, km, vm, tq):
    b, s, _ = qm.shape
    tq = min(tq, s)
    qt, kt = _tri_tables(s // tq)
    qmap = lambda bb, j, t, qt_, kt_: (bb, qt_[t], j)
    kmap = lambda bb, j, t, qt_, kt_: (bb, kt_[t], j)
    return pl.pallas_call(
        functools.partial(_mla_prefill_kernel, tq=tq),
        out_shape=jax.ShapeDtypeStruct((b, s, MLA_HEADS * MLA_VDIM), BF16),
        grid_spec=pltpu.PrefetchScalarGridSpec(
            num_scalar_prefetch=2,
            grid=(b, MLA_HEADS // 2, qt.shape[0]),
            in_specs=[pl.BlockSpec((1, tq, 2 * LANES), qmap),
                      pl.BlockSpec((1, tq, 2 * LANES), kmap),
                      pl.BlockSpec((1, tq, LANES), kmap)],
            out_specs=pl.BlockSpec((1, tq, LANES), qmap),
            scratch_shapes=[pltpu.VMEM((2, tq, LANES), F32)] * 3),
        name="mla_prefill",
        compiler_params=_params(("arbitrary",) * 3, 48),
    )(qt, kt, qm, km, vm)


def _compress_kernel(pt_ref, cache_ref, wt_ref, pe_ref, b1_ref, w2k_ref, w2v_ref, gk_ref,
                     kc_ref, vc_ref, bufk, bufv, sem, acc, *, pages, nchunk, lane0):
    s = pl.program_id(0)
    c = pl.program_id(1)
    step = s * nchunk + c
    nsteps = pl.num_programs(0) * nchunk
    slot = step % 2
    page_rows = 2 * CMP_BLOCK

    def copies(seq, chunk, slot_, start):
        def one(p, carry):
            page = pt_ref[seq, chunk * pages + p]
            row0 = pl.multiple_of((slot_ * pages + p) * page_rows, page_rows)
            for kv, dst in enumerate((bufk, bufv)):
                cp = pltpu.make_async_copy(cache_ref.at[page, :, pl.ds(lane0 + LANES * kv, LANES)],
                                           dst.at[pl.ds(row0, page_rows), :], sem.at[slot_, kv])
                if start:
                    cp.start()
                else:
                    cp.wait()
            return carry
        lax.fori_loop(0, pages, one, 0, unroll=8)

    @pl.when(step == 0)
    def _():
        copies(s, c, slot, True)

    @pl.when(step + 1 < nsteps)
    def _():
        nxt = step + 1
        copies(nxt // nchunk, nxt % nchunk, 1 - slot, True)

    copies(s, c, slot, False)
    acc[...] = jnp.zeros(acc.shape, F32)
    base = slot * pages * page_rows

    def rbody(r, carry):
        def rows_of(src):
            return jnp.concatenate([src[pl.ds(base + r, pages, stride=page_rows), :],
                                    src[pl.ds(base + CMP_BLOCK + r, pages, stride=page_rows), :]], axis=0)
        xr = (jnp.concatenate([rows_of(bufk), rows_of(bufv)], axis=1) + pe_ref[r]).astype(BF16)
        acc[...] += _nt(wt_ref[r], xr)
        return carry
    lax.fori_loop(0, CMP_BLOCK, rbody, 0)

    hid = jax.nn.gelu(acc[...] + b1_ref[...])
    kc = _nn(w2k_ref[...], hid[:2 * CMP_HID].astype(BF16))
    vc = _nn(w2v_ref[...], hid[2 * CMP_HID:].astype(BF16))
    gk = gk_ref[...]
    parts = []
    for g in range(NSA_GROUPS):
        part = kc[NSA_DH * g:NSA_DH * (g + 1)]
        ss = jnp.sum(part * part, axis=0, keepdims=True)
        parts.append(part * lax.rsqrt(ss / NSA_DH + EPS) * gk[NSA_DH * g:NSA_DH * (g + 1)])
    kc_ref[0] = jnp.concatenate(parts, axis=0).astype(BF16)
    vc_ref[0] = vc.astype(BF16)


def _compress(page_table, cache3d, w, pages_per_step):
    nseq, npages = page_table.shape
    pages = min(pages_per_step, npages)
    assert npages % pages == 0
    nchunk = npages // pages
    nblk = 2 * pages
    full = lambda a: pl.BlockSpec(a.shape, lambda s, c, pt: (0,) * a.ndim)
    consts = [w["cmp_wt"], w["cmp_pe"], w["cmp_b1"], w["cmp_w2k"], w["cmp_w2v"], w["g_k_cmp_col"]]
    out_spec = pl.BlockSpec((1, LANES, nblk), lambda s, c, pt: (s, 0, c))
    kc, vc = pl.pallas_call(
        functools.partial(_compress_kernel, pages=pages, nchunk=nchunk, lane0=0),
        out_shape=[jax.ShapeDtypeStruct((nseq, LANES, nchunk * nblk), BF16)] * 2,
        grid_spec=pltpu.PrefetchScalarGridSpec(
            num_scalar_prefetch=1,
            grid=(nseq, nchunk),
            in_specs=[pl.BlockSpec(memory_space=pl.ANY)] + [full(a) for a in consts],
            out_specs=[out_spec, out_spec],
            scratch_shapes=[pltpu.VMEM((2 * pages * 2 * CMP_BLOCK, LANES), F32),
                            pltpu.VMEM((2 * pages * 2 * CMP_BLOCK, LANES), F32),
                            pltpu.SemaphoreType.DMA((2, 2)),
                            pltpu.VMEM((4 * CMP_HID, nblk), F32)]),
        name="compress_prompt",
        compiler_params=_params(("arbitrary", "arbitrary"), 56),
    )(page_table, cache3d, *consts)

    def unpermute(a):
        a = a.reshape(nseq, LANES, nchunk, 2, pages)
        return jnp.transpose(a, (0, 1, 2, 4, 3)).reshape(nseq, LANES, nchunk * nblk)
    return unpermute(kc), unpermute(vc)


def _cmp_prefill_kernel(q_ref, kc_ref, vc_ref, cols_ref, gates_ref, o_ref, selb_ref, *, tq, nb, nbp):
    qi = pl.program_id(1)
    kc = kc_ref[0]
    vc = vc_ref[0]
    cols = cols_ref[...]
    gates = gates_ref[0]
    t = lax.broadcasted_iota(jnp.int32, (tq, nb), 0) + qi * tq
    j = lax.broadcasted_iota(jnp.int32, (tq, nb), 1)
    cb = t // CMP_BLOCK
    r = t % CMP_BLOCK
    u = cb - j
    vis = (u >= 1) | ((u == 0) & (r == CMP_BLOCK - 1))
    visf = jnp.where(vis, 1.0, 0.0)
    lo = _lane_iota((tq, LANES)) < HALF
    imp = [jnp.zeros((tq, nb), F32), jnp.zeros((tq, nb), F32)]
    outs = []
    for jp in range(NSA_HPG):
        qv = q_ref[0, :, LANES * jp:LANES * (jp + 1)]
        halves = []
        for half in range(NSA_GROUPS):
            hd = jp + NSA_HPG * half
            qh = jnp.where(lo if half == 0 else jnp.logical_not(lo), qv, jnp.zeros_like(qv))
            s = _nn(qh, kc)
            bias = jnp.where(u == 1, cols[:, hd:hd + 1],
                             jnp.where(u == 2, cols[:, 8 + hd:9 + hd],
                                       jnp.where(u == 0, cols[:, 16 + hd:17 + hd], 0.0)))
            s = jnp.where(vis, s + bias, NEG)
            m = jnp.max(s, axis=1, keepdims=True)
            p = jnp.exp(s - m) * visf
            p = p / jnp.maximum(jnp.sum(p, axis=1, keepdims=True), 1e-30)
            imp[half] = imp[half] + p
            halves.append(_nt(p.astype(BF16), vc) * gates[:, hd:hd + 1])
        outs.append(jnp.where(lo, halves[0], halves[1]))
    o_ref[0] = jnp.concatenate(outs, axis=1)
    cand = j < cb
    scores = jnp.concatenate([jnp.where(cand, imp[g], -1.0) for g in range(NSA_GROUPS)], axis=0)
    sel_all = _topk_select(scores, N_SEL - 1)
    sels = []
    for g in range(NSA_GROUPS):
        sb = jnp.where((sel_all[tq * g:tq * (g + 1)] > 0.0) | (j == cb), 0.0, NEG)
        if nbp > nb:
            sb = jnp.concatenate([sb, jnp.full((tq, nbp - nb), NEG, F32)], axis=1)
        sels.append(sb)
    selb_ref[0] = jnp.concatenate(sels, axis=1).astype(BF16)


def _cmp_prefill(nq, kct, vct, cols, gates, tq):
    b, s, _ = nq.shape
    tq = min(tq, s)
    nb = kct.shape[2]
    nbp = -(-nb // LANES) * LANES
    return pl.pallas_call(
        functools.partial(_cmp_prefill_kernel, tq=tq, nb=nb, nbp=nbp),
        out_shape=[jax.ShapeDtypeStruct((b, s, 4 * LANES), F32),
                   jax.ShapeDtypeStruct((b, s, NSA_GROUPS * nbp), BF16)],
        grid=(b, s // tq),
        in_specs=[pl.BlockSpec((1, tq, 4 * LANES), lambda bb, qi: (bb, qi, 0)),
                  pl.BlockSpec((1, LANES, nb), lambda bb, qi: (bb, 0, 0)),
                  pl.BlockSpec((1, LANES, nb), lambda bb, qi: (bb, 0, 0)),
                  pl.BlockSpec((tq, LANES), lambda bb, qi: (0, 0)),
                  pl.BlockSpec((1, tq, LANES), lambda bb, qi: (bb, qi, 0))],
        out_specs=[pl.BlockSpec((1, tq, 4 * LANES), lambda bb, qi: (bb, qi, 0)),
                   pl.BlockSpec((1, tq, NSA_GROUPS * nbp), lambda bb, qi: (bb, qi, 0))],
        name="cmp_prefill",
        compiler_params=_params(("arbitrary", "arbitrary"), 48),
    )(nq, kct, vct, cols, gates)


def _slc_prefill_kernel(qt_ref, kt_ref, q_ref, selb_ref, k_ref, v_ref, td_ref, ts_ref, gates_ref, o_ref,
                        m_sc, l_sc, acc_sc, *, tq, nbp):
    jp = pl.program_id(1)
    qi = qt_ref[pl.program_id(2)]
    ki = kt_ref[pl.program_id(2)]
    nsub = tq // SUB

    @pl.when(ki == 0)
    def _():
        m_sc[...] = jnp.full(m_sc.shape, NEG, F32)
        l_sc[...] = jnp.zeros(l_sc.shape, F32)
        acc_sc[...] = jnp.zeros(acc_sc.shape, F32)

    def add_bias(s, half, kind):
        if kind == 0:
            return s
        td = td_ref[jp + NSA_HPG * half]
        ts = ts_ref[jp + NSA_HPG * half]
        rows = []
        for a in range(nsub):
            row = s[SUB * a:SUB * (a + 1), :]
            pcs = [row[:, SUB * c:SUB * (c + 1)] for c in range(nsub)]
            if kind == 2:
                pcs[a] = pcs[a] + td
                if a >= 1:
                    pcs[a - 1] = pcs[a - 1] + ts
            elif a == 0:
                pcs[nsub - 1] = pcs[nsub - 1] + ts
            rows.append(jnp.concatenate(pcs, axis=1) if nsub > 1 else pcs[0])
        return jnp.concatenate(rows, axis=0) if nsub > 1 else rows[0]

    def body(kind):
        k = k_ref[0]
        v = v_ref[0]
        kpos = lax.broadcasted_iota(jnp.int32, (tq, nbp), 0) + ki * tq
        blk = lax.broadcasted_iota(jnp.int32, (tq, nbp), 1)
        onehot = jnp.where(kpos // CMP_BLOCK == blk, 1.0, 0.0).astype(BF16)
        kaug = jnp.concatenate([k, onehot], axis=1)
        lo = _lane_iota((tq, LANES)) < HALF
        qv = q_ref[0]
        for half in range(NSA_GROUPS):
            qh = jnp.where(lo if half == 0 else jnp.logical_not(lo), qv, jnp.zeros_like(qv))
            qaug = jnp.concatenate([qh, selb_ref[0, :, nbp * half:nbp * (half + 1)]], axis=1)
            _online_update(add_bias(_nt(qaug, kaug), half, kind), v, m_sc, l_sc, acc_sc, half)

    @pl.when(ki < qi - 1)
    def _():
        body(0)

    @pl.when(ki == qi - 1)
    def _():
        body(1)

    @pl.when(ki == qi)
    def _():
        body(2)
        gates = gates_ref[0]
        lane = _lane_iota(gates.shape)
        g0 = jnp.sum(jnp.where(lane == NSA_HEADS + jp, gates, 0.0), axis=1, keepdims=True)
        g1 = jnp.sum(jnp.where(lane == NSA_HEADS + NSA_HPG + jp, gates, 0.0), axis=1, keepdims=True)
        lo = _lane_iota((tq, LANES)) < HALF
        o_ref[0] = jnp.where(lo, acc_sc[0] / l_sc[0] * g0, acc_sc[1] / l_sc[1] * g1)


def _slc_prefill(nq, selb, slck, slcv, td, ts, gates, tq):
    b, s, _ = nq.shape
    tq = min(tq, s)
    nbp = selb.shape[2] // NSA_GROUPS
    qt, kt = _tri_tables(s // tq)
    qmap = lambda bb, j, t, qt_, kt_: (bb, qt_[t], 0)
    kmap = lambda bb, j, t, qt_, kt_: (bb, kt_[t], 0)
    const = lambda bb, j, t, qt_, kt_: (0, 0, 0)
    return pl.pallas_call(
        functools.partial(_slc_prefill_kernel, tq=tq, nbp=nbp),
        out_shape=jax.ShapeDtypeStruct((b, s, 4 * LANES), F32),
        grid_spec=pltpu.PrefetchScalarGridSpec(
            num_scalar_prefetch=2,
            grid=(b, NSA_HPG, qt.shape[0]),
            in_specs=[pl.BlockSpec((1, tq, LANES), lambda bb, j, t, qt_, kt_: (bb, qt_[t], j)),
                      pl.BlockSpec((1, tq, NSA_GROUPS * nbp), qmap),
                      pl.BlockSpec((1, tq, LANES), kmap),
                      pl.BlockSpec((1, tq, LANES), kmap),
                      pl.BlockSpec((NSA_HEADS, SUB, SUB), const),
                      pl.BlockSpec((NSA_HEADS, SUB, SUB), const),
                      pl.BlockSpec((1, tq, LANES), qmap)],
            out_specs=pl.BlockSpec((1, tq, LANES), lambda bb, j, t, qt_, kt_: (bb, qt_[t], j)),
            scratch_shapes=[pltpu.VMEM((NSA_GROUPS, tq, LANES), F32)] * 3),
        name="slc_prefill",
        compiler_params=_params(("arbitrary",) * 3, 48),
    )(qt, kt, nq, selb, slck, slcv, td, ts, gates)


def _win_prefill_kernel(q_ref, *refs, tq, nkb):
    k_refs = refs[:nkb]
    v_refs = refs[nkb:2 * nkb]
    bias_ref, gates_ref, o_ref = refs[2 * nkb:]
    qi = pl.program_id(1)
    kcat = jnp.concatenate([kr[0] for kr in k_refs], axis=0)
    vcat = jnp.concatenate([vr[0] for vr in v_refs], axis=0)
    gates = gates_ref[0]
    col_blk = lax.broadcasted_iota(jnp.int32, (tq, nkb * tq), 1) // tq
    in_seq = col_blk + qi >= nkb - 1
    lo = _lane_iota((tq, LANES)) < HALF
    outs = []
    for jp in range(NSA_HPG):
        qv = q_ref[0, :, LANES * jp:LANES * (jp + 1)]
        halves = []
        for half in range(NSA_GROUPS):
            hd = jp + NSA_HPG * half
            qh = jnp.where(lo if half == 0 else jnp.logical_not(lo), qv, jnp.zeros_like(qv))
            s = jnp.where(in_seq, _nt(qh, kcat) + bias_ref[hd], NEG)
            m = jnp.max(s, axis=1, keepdims=True)
            p = jnp.exp(s - m)
            inv = gates[:, 16 + hd:17 + hd] / jnp.sum(p, axis=1, keepdims=True)
            halves.append(_nn(p.astype(BF16), vcat) * inv)
        outs.append(jnp.where(lo, halves[0], halves[1]))
    o_ref[0] = jnp.concatenate(outs, axis=1)


def _win_prefill(nq, wink, winv, bias, gates, tq):
    b, s, _ = nq.shape
    nkb = bias.shape[2] // bias.shape[1]
    assert bias.shape[1] == tq and s % tq == 0
    qmap = lambda bb, qi: (bb, qi, 0)
    kmaps = [(lambda bb, qi, o=o: (bb, jnp.maximum(qi - o, 0), 0)) for o in range(nkb - 1, -1, -1)]
    kspecs = [pl.BlockSpec((1, tq, LANES), km) for km in kmaps]
    return pl.pallas_call(
        functools.partial(_win_prefill_kernel, tq=tq, nkb=nkb),
        out_shape=jax.ShapeDtypeStruct((b, s, 4 * LANES), F32),
        grid=(b, s // tq),
        in_specs=[pl.BlockSpec((1, tq, 4 * LANES), qmap)] + kspecs + kspecs +
                 [pl.BlockSpec(bias.shape, lambda bb, qi: (0, 0, 0)), pl.BlockSpec((1, tq, LANES), qmap)],
        out_specs=pl.BlockSpec((1, tq, 4 * LANES), qmap),
        name="win_prefill",
        compiler_params=_params(("arbitrary", "arbitrary"), 56),
    )(nq, *([wink] * nkb), *([winv] * nkb), bias, gates)


def _merge_kernel(x_ref, omla_ref, oc_ref, os_ref, ow_ref, mg_ref, wum_ref, wun_ref, wout_ref,
                  gffn_ref, wr_ref, br_ref, x1_ref, h2_ref, comb_ref, *, d_model):
    o_nsa = (oc_ref[...] + os_ref[...]) + ow_ref[...]
    ym = _nn(omla_ref[...], wum_ref[...])
    yn = _nn(o_nsa.astype(BF16), wun_ref[...])
    mg = mg_ref[...]
    y = mg[:, :d_model] * ym + mg[:, d_model:] * yn
    x1 = x_ref[...] + _nn(y.astype(BF16), wout_ref[...])
    x1_ref[...] = x1
    h2 = x1 * lax.rsqrt(jnp.mean(x1 * x1, axis=-1, keepdims=True) + EPS) * gffn_ref[...]
    h2b = h2.astype(BF16)
    h2_ref[...] = h2b
    logits = _nn(h2b, wr_ref[...]) + br_ref[...]
    lane = _lane_iota(logits.shape).astype(F32)
    is_g = (lane >= N_EXPERTS) & (lane < N_EXPERTS + N_GROUPS)
    gl = jnp.where(is_g, logits, NEG)
    gmax = jnp.max(gl, axis=1, keepdims=True)
    gsel = jnp.min(jnp.where(gl == gmax, lane, 1e9), axis=1, keepdims=True) - N_EXPERTS
    gw = 1.0 / jnp.sum(jnp.exp(gl - gmax), axis=1, keepdims=True)
    ing = (lane >= gsel * EXPERTS_PER_GROUP) & (lane < (gsel + 1.0) * EXPERTS_PER_GROUP)
    el = jnp.where(ing, logits, NEG)
    ex = jnp.exp(el - jnp.max(el, axis=1, keepdims=True))
    p = ex / jnp.sum(ex, axis=1, keepdims=True)
    pin = jnp.where(ing, p, -1.0)
    p1 = jnp.max(pin, axis=1, keepdims=True)
    i1 = jnp.min(jnp.where(pin == p1, lane, 1e9), axis=1, keepdims=True)
    prest = jnp.where(lane == i1, -1.0, pin)
    p2 = jnp.max(prest, axis=1, keepdims=True)
    i2 = jnp.min(jnp.where(prest == p2, lane, 1e9), axis=1, keepdims=True)
    tot = p1 + p2
    comb_ref[...] = jnp.where(lane == i1, gw * p1 / tot, jnp.where(lane == i2, gw * p2 / tot, 0.0))


def _merge(x2d, omla, oc, osl, ow, mg, w, tm):
    t, d = x2d.shape
    tm = min(tm, t)
    full = lambda a: pl.BlockSpec(a.shape, lambda i: (0,) * a.ndim)
    tok = lambda n: pl.BlockSpec((tm, n), lambda i: (i, 0))
    weights = [w["w_up_mla"], w["w_up_nsa"], w["w_out"], w["g_ffn"], w["w_router"], w["b_router"]]
    return pl.pallas_call(
        functools.partial(_merge_kernel, d_model=d),
        out_shape=[jax.ShapeDtypeStruct((t, d), F32), jax.ShapeDtypeStruct((t, d), BF16),
                   jax.ShapeDtypeStruct((t, LANES), F32)],
        grid=(t // tm,),
        in_specs=[tok(d), tok(4 * LANES), tok(4 * LANES), tok(4 * LANES), tok(4 * LANES), tok(2 * d)] +
                 [full(a) for a in weights],
        out_specs=[tok(d), tok(d), tok(LANES)],
        name="merge_router",
        compiler_params=_params(("arbitrary",), 48),
    )(x2d, omla, oc, osl, ow, mg, *weights)


def _moe_kernel(h2_ref, x1_ref, comb_ref, wgu_ref, wd_ref, o_ref, *, experts):
    step = pl.program_id(1)

    @pl.when(step == 0)
    def _():
        o_ref[...] = x1_ref[...]

    h = h2_ref[...]
    comb = comb_ref[...]
    lane = _lane_iota(comb.shape)
    out = None
    for i in range(experts):
        gu = _nn(h, wgu_ref[i])
        hid = jax.nn.silu(gu[:, :D_EXPERT]) * gu[:, D_EXPERT:]
        ce = jnp.sum(jnp.where(lane == step * experts + i, comb, 0.0), axis=1, keepdims=True)
        y = _nn((hid * ce).astype(BF16), wd_ref[i])
        out = y if out is None else out + y
    o_ref[...] += out


def _moe(h2, x1, comb, w, tm, experts):
    t, d = x1.shape
    tm = min(tm, t)
    assert N_EXPERTS % experts == 0
    tok = lambda n: pl.BlockSpec((tm, n), lambda i, e: (i, 0))
    return pl.pallas_call(
        functools.partial(_moe_kernel, experts=experts),
        out_shape=jax.ShapeDtypeStruct((t, d), F32),
        grid=(t // tm, N_EXPERTS // experts),
        in_specs=[tok(d), tok(d), tok(LANES),
                  pl.BlockSpec((experts, d, 2 * D_EXPERT), lambda i, e: (e, 0, 0)),
                  pl.BlockSpec((experts, D_EXPERT, d), lambda i, e: (e, 0, 0))],
        out_specs=tok(d),
        name="moe",
        compiler_params=_params(("arbitrary", "arbitrary"), 48),
    )(h2, x1, comb, w["w_gu"], w["w_down"])


def _mla_qabs_kernel(q_ref, g_ref, w_ref, o_ref):
    g = g_ref[...]
    for hd in range(MLA_HEADS):
        qg = (q_ref[:, LANES * hd:LANES * (hd + 1)] * g).astype(BF16)
        o_ref[:, 2 * LANES * hd:2 * LANES * (hd + 1)] = _nn(qg, w_ref[hd]).astype(BF16)


def _mla_qabs(qmf, w):
    t = qmf.shape[0]
    return pl.pallas_call(
        _mla_qabs_kernel,
        out_shape=jax.ShapeDtypeStruct((t, MLA_HEADS * 2 * LANES), BF16),
        grid=(1,),
        in_specs=[pl.BlockSpec(qmf.shape, lambda i: (0, 0)),
                  pl.BlockSpec((1, LANES), lambda i: (0, 0)),
                  pl.BlockSpec(w["w_abs"].shape, lambda i: (0, 0, 0))],
        out_specs=pl.BlockSpec((t, MLA_HEADS * 2 * LANES), lambda i: (0, 0)),
        name="mla_q_absorb",
        compiler_params=_params(("arbitrary",), 32),
    )(qmf, w["g_mla_k"], w["w_abs"])


def _page_copies(pt_ref, cache_ref, buf, sem, seq, first_page, slot, pages, feat0, feats, start):
    def one(p, carry):
        page = pt_ref[seq, first_page + p]
        row0 = pl.multiple_of((slot * pages + p) * feats, 8)
        cp = pltpu.make_async_copy(cache_ref.at[page, pl.ds(feat0, feats), :],
                                   buf.at[pl.ds(row0, feats), :], sem.at[slot])
        if start:
            cp.start()
        else:
            cp.wait()
        return carry
    lax.fori_loop(0, pages, one, 0, unroll=8)


def _paged_prefetch(pt_ref, cache_ref, buf, sem, pages, nchunk, feat0, feats):
    s = pl.program_id(0)
    c = pl.program_id(1)
    step = s * nchunk + c
    slot = step % 2
    copies = functools.partial(_page_copies, pt_ref, cache_ref, buf, sem, pages=pages, feat0=feat0, feats=feats)

    @pl.when(step == 0)
    def _():
        copies(seq=s, first_page=c * pages, slot=slot, start=True)

    @pl.when(step + 1 < pl.num_programs(0) * nchunk)
    def _():
        nxt = step + 1
        copies(seq=nxt // nchunk, first_page=(nxt % nchunk) * pages, slot=1 - slot, start=True)

    copies(seq=s, first_page=c * pages, slot=slot, start=False)
    return slot


def _key_tile(buf, slot, pages, feats, first, count, f0, nf):
    parts = []
    for j in range(count):
        row0 = pl.multiple_of((slot * pages + first + j) * feats + f0, 8)
        parts.append(buf[pl.ds(row0, nf), :])
    return jnp.concatenate(parts, axis=1) if count > 1 else parts[0]


def _mla_decode_kernel(pt_ref, cache_ref, qabs_ref, wkt_ref, newrows_ref, newmask_ref, o_ref,
                       buf, sem, m_sc, l_sc, acc_sc, *, pages, nchunk, ppsub, dec_seq):
    c = pl.program_id(1)
    slot = _paged_prefetch(pt_ref, cache_ref, buf, sem, pages, nchunk, 0, MLA_ROW)
    nk = wkt_ref.shape[0]

    @pl.when(c == 0)
    def _():
        m_sc[...] = jnp.full(m_sc.shape, NEG, F32)
        l_sc[...] = jnp.zeros(l_sc.shape, F32)
        acc_sc[...] = jnp.zeros(acc_sc.shape, F32)

    lhs = jnp.concatenate([wkt_ref[...], qabs_ref[0][:, :MLA_ROW]], axis=0)

    def attend(rows_t, mask):
        rb = rows_t.astype(BF16)
        big = _nn(lhs, rb)
        sq = big[:nk] * big[:nk]
        npe = MLA_HEADS * MLA_NOPE
        ss = jnp.sum(sq[:npe].reshape(MLA_HEADS, MLA_NOPE, sq.shape[1]), axis=1)
        ss = ss + jnp.sum(sq[npe:], axis=0, keepdims=True)
        rinv = lax.rsqrt(ss / MLA_QK + EPS)
        st = big[nk:]
        sc = jnp.concatenate([st[MLA_HEADS * tk:MLA_HEADS * (tk + 1)] * rinv for tk in range(dec_seq)], axis=0)
        if mask is not None:
            sc = sc + mask
        _online_update(sc, rb[:KV_LORA], m_sc, l_sc, acc_sc, 0, v_transposed=True)

    for i in range(pages // ppsub):
        attend(_key_tile(buf, slot, pages, MLA_ROW, i * ppsub, ppsub, 0, MLA_ROW), None)

    @pl.when(c == nchunk - 1)
    def _():
        attend(newrows_ref[0], newmask_ref[...])
        o_ref[0] = acc_sc[0] / l_sc[0]


def _mla_decode(page_table, cache_t, qabs, newrows_t, newmask, w, pages_per_step, ppsub, dec_seq):
    nseq, npages = page_table.shape
    pages = min(pages_per_step, npages)
    ppsub = min(ppsub, pages)
    assert npages % pages == 0 and pages % ppsub == 0
    nchunk = npages // pages
    rows = MLA_HEADS * dec_seq
    seqmap = lambda s, c, pt: (s, 0, 0)
    return pl.pallas_call(
        functools.partial(_mla_decode_kernel, pages=pages, nchunk=nchunk, ppsub=ppsub, dec_seq=dec_seq),
        out_shape=jax.ShapeDtypeStruct((nseq, rows, LANES), F32),
        grid_spec=pltpu.PrefetchScalarGridSpec(
            num_scalar_prefetch=1,
            grid=(nseq, nchunk),
            in_specs=[pl.BlockSpec(memory_space=pl.ANY),
                      pl.BlockSpec((1, rows, 2 * LANES), seqmap),
                      pl.BlockSpec(w["w_kt_aug"].shape, lambda s, c, pt: (0, 0)),
                      pl.BlockSpec((1, MLA_ROW, LANES), seqmap),
                      pl.BlockSpec(newmask.shape, lambda s, c, pt: (0, 0))],
            out_specs=pl.BlockSpec((1, rows, LANES), seqmap),
            scratch_shapes=[pltpu.VMEM((2 * pages * MLA_ROW, LANES), F32),
                            pltpu.SemaphoreType.DMA((2,)),
                            pltpu.VMEM((1, rows, LANES), F32),
                            pltpu.VMEM((1, rows, LANES), F32),
                            pltpu.VMEM((1, rows, LANES), F32)]),
        name="mla_decode",
        compiler_params=_params(("arbitrary", "arbitrary"), 48),
    )(page_table, cache_t, qabs, w["w_kt_aug"], newrows_t, newmask)


def _mla_vout_kernel(o_ref, w_ref, out_ref):
    for jp in range(MLA_HEADS // 2):
        pair = o_ref[:, 2 * LANES * jp:2 * LANES * (jp + 1)].astype(BF16)
        out_ref[:, LANES * jp:LANES * (jp + 1)] = _nn(pair, w_ref[jp]).astype(BF16)


def _mla_vout(olat, w):
    t = olat.shape[0]
    return pl.pallas_call(
        _mla_vout_kernel,
        out_shape=jax.ShapeDtypeStruct((t, MLA_HEADS * MLA_VDIM), BF16),
        grid=(1,),
        in_specs=[pl.BlockSpec(olat.shape, lambda i: (0, 0)),
                  pl.BlockSpec(w["w_v_pair"].shape, lambda i: (0, 0, 0))],
        out_specs=pl.BlockSpec((t, MLA_HEADS * MLA_VDIM), lambda i: (0, 0)),
        name="mla_v_up",
        compiler_params=_params(("arbitrary",), 32),
    )(olat, w["w_v_pair"])


def _compress_t_kernel(pt_ref, cache_ref, wk_ref, wv_ref, pe_ref, b1_ref, w2k_ref, w2v_ref, gk_ref,
                       kc_ref, vc_ref, buf, sem, acc, *, pages, nchunk, dchunk):
    feats = 2 * LANES
    s = pl.program_id(0)
    c = pl.program_id(1)
    step = s * nchunk + c
    slot = step % 2

    def copies(seq, chunk, slot_, start):
        def one(p, carry):
            page = pt_ref[seq, chunk * pages + p]
            cp = pltpu.make_async_copy(cache_ref.at[page, pl.ds(0, feats), :], buf.at[slot_, :, p, :], sem.at[slot_])
            if start:
                cp.start()
            else:
                cp.wait()
            return carry
        lax.fori_loop(0, pages, one, 0, unroll=8)

    @pl.when(step == 0)
    def _():
        copies(s, c, slot, True)

    @pl.when(step + 1 < pl.num_programs(0) * nchunk)
    def _():
        nxt = step + 1
        copies(nxt // nchunk, nxt % nchunk, 1 - slot, True)

    copies(s, c, slot, False)

    outs = []
    for cc, (w1_ref, w2_ref) in enumerate(((wk_ref, w2k_ref), (wv_ref, w2v_ref))):
        acc[...] = jnp.zeros(acc.shape, F32)
        for d0 in range(0, NSA_DH, dchunk):
            cols = [jnp.concatenate([buf[slot, cc * LANES + g * NSA_DH + d] for g in range(NSA_GROUPS)], axis=0)
                    for d in range(d0, d0 + dchunk)]
            ksl = slice(d0 * LANES, (d0 + dchunk) * LANES)
            lhs = jnp.concatenate(cols, axis=1) + pe_ref[cc, :, ksl]
            acc[...] += _nn(lhs.astype(BF16), w1_ref[ksl, :])
        hid = jax.nn.gelu(acc[...] + b1_ref[cc])
        outs.append(_nn(hid.astype(BF16), w2_ref[...]))
    kc_ref[0] = _pair_rms(outs[0], gk_ref[...]).reshape(NSA_GROUPS, pages, LANES).astype(BF16)
    vc_ref[0] = outs[1].reshape(NSA_GROUPS, pages, LANES).astype(BF16)


def _compress_t(page_table, cache_t, w, pages_per_step):
    nseq, npages = page_table.shape
    pages = min(pages_per_step, npages)
    assert npages % pages == 0
    nchunk = npages // pages
    const = lambda a: pl.BlockSpec(a.shape, lambda s, c, pt: (0,) * a.ndim)
    consts = [w["cmp_w1k_t"], w["cmp_w1v_t"], w["cmp_pe_t"], w["cmp_b1_t"], w["cmp_w2k_t"], w["cmp_w2v_t"],
              w["g_k_cmp"]]
    out_spec = pl.BlockSpec((1, NSA_GROUPS, pages, LANES), lambda s, c, pt: (s, 0, c, 0))
    kc, vc = pl.pallas_call(
        functools.partial(_compress_t_kernel, pages=pages, nchunk=nchunk, dchunk=8),
        out_shape=[jax.ShapeDtypeStruct((nseq, NSA_GROUPS, npages, LANES), BF16)] * 2,
        grid_spec=pltpu.PrefetchScalarGridSpec(
            num_scalar_prefetch=1,
            grid=(nseq, nchunk),
            in_specs=[pl.BlockSpec(memory_space=pl.ANY)] + [const(a) for a in consts],
            out_specs=[out_spec, out_spec],
            scratch_shapes=[pltpu.VMEM((2, 2 * LANES, pages, LANES), F32),
                            pltpu.SemaphoreType.DMA((2,)),
                            pltpu.VMEM((NSA_GROUPS * pages, 2 * CMP_HID), F32)]),
        name="compress_sample",
        compiler_params=_params(("arbitrary", "arbitrary"), 56),
    )(page_table, cache_t, *consts)

    def pair_pack(a):
        a = a.reshape(nseq, NSA_GROUPS, 2 * npages, NSA_DH)
        return jnp.transpose(a, (0, 2, 1, 3)).reshape(nseq, 2 * npages, LANES)
    return pair_pack(kc), pair_pack(vc)


def _group_rinv(kt, dec_rows):
    ksq = kt * kt
    r0 = lax.rsqrt(jnp.sum(ksq[:NSA_DH], axis=0, keepdims=True) / NSA_DH + EPS)
    r1 = lax.rsqrt(jnp.sum(ksq[NSA_DH:], axis=0, keepdims=True) / NSA_DH + EPS)
    first = lax.broadcasted_iota(jnp.int32, (dec_rows, 1), 0) < dec_rows // NSA_GROUPS
    return jnp.where(first, r0, r1)


def _scale_rows(st, rinv, dec_rows):
    return jnp.concatenate([st[dec_rows * r:dec_rows * (r + 1)] * rinv for r in range(NSA_HPG)], axis=0)


def _cmp_decode_kernel(q_ref, kc_ref, vc_ref, bias_ref, cand_ref, gates_ref, o_ref, selb_ref, *, dec_rows):
    s = _nt(q_ref[0], kc_ref[0]) + bias_ref[...]
    vis = bias_ref[...] > 0.5 * NEG
    m = jnp.max(s, axis=1, keepdims=True)
    p = jnp.exp(s - m) * jnp.where(vis, 1.0, 0.0)
    p = p / jnp.maximum(jnp.sum(p, axis=1, keepdims=True), 1e-30)
    o_ref[0] = _nn(p.astype(BF16), vc_ref[0]) * gates_ref[0][:, 0:1]
    imp = p[0:dec_rows]
    for r in range(1, NSA_HPG):
        imp = imp + p[dec_rows * r:dec_rows * (r + 1)]
    sel = _topk_select(jnp.where(cand_ref[...] > 0.5, imp, -1.0), N_SEL - 1)
    selb_ref[0] = jnp.where(sel > 0.0, 0.0, NEG).astype(BF16)


def _cmp_decode(q32, kc, vc, bias, cand, gates32):
    nseq, rows, _ = q32.shape
    nbs = kc.shape[1]
    dec_rows = rows // NSA_HPG
    seqmap = lambda s: (s, 0, 0)
    return pl.pallas_call(
        functools.partial(_cmp_decode_kernel, dec_rows=dec_rows),
        out_shape=[jax.ShapeDtypeStruct((nseq, rows, LANES), F32),
                   jax.ShapeDtypeStruct((nseq, dec_rows, nbs), BF16)],
        grid=(nseq,),
        in_specs=[pl.BlockSpec((1, rows, LANES), seqmap),
                  pl.BlockSpec((1, nbs, LANES), seqmap),
                  pl.BlockSpec((1, nbs, LANES), seqmap),
                  pl.BlockSpec(bias.shape, lambda s: (0, 0)),
                  pl.BlockSpec(cand.shape, lambda s: (0, 0)),
                  pl.BlockSpec((1, rows, LANES), seqmap)],
        out_specs=[pl.BlockSpec((1, rows, LANES), seqmap), pl.BlockSpec((1, dec_rows, nbs), seqmap)],
        name="cmp_decode",
        compiler_params=_params(("arbitrary",), 32),
    )(q32, kc, vc, bias, cand, gates32)


def _slc_decode_kernel(pt_ref, cache_ref, q_ref, mrows_ref, eb_ref, new_ref, newbias_ref,
                       lastbias_ref, gates_ref, o_ref, buf, sem, m_sc, l_sc, acc_sc, mask_sc,
                       *, pages, nchunk, ppsub, dec_rows):
    c = pl.program_id(1)
    feats = 2 * LANES
    slot = _paged_prefetch(pt_ref, cache_ref, buf, sem, pages, nchunk, 2 * LANES, feats)
    nsub = pages // ppsub

    @pl.when(c == 0)
    def _():
        m_sc[...] = jnp.full(m_sc.shape, NEG, F32)
        l_sc[...] = jnp.zeros(l_sc.shape, F32)
        acc_sc[...] = jnp.zeros(acc_sc.shape, F32)
        mask_sc[...] = _nn(mrows_ref[0], eb_ref[...])

    q = q_ref[0]

    def attend(kt, vt, extra):
        st = _nn(q, kt.astype(BF16))
        sc = _scale_rows(st, _group_rinv(kt, dec_rows), dec_rows) + extra
        _online_update(sc, vt.astype(BF16), m_sc, l_sc, acc_sc, 0, v_transposed=True)

    for i in range(nsub):
        kt = _key_tile(buf, slot, pages, feats, i * ppsub, ppsub, 0, LANES)
        vt = _key_tile(buf, slot, pages, feats, i * ppsub, ppsub, LANES, LANES)
        gsub = c * nsub + i
        mk = mask_sc[pl.ds(pl.multiple_of(gsub * dec_rows, dec_rows), dec_rows), :]
        is_last = jnp.where(gsub == nchunk * nsub - 1, 1.0, 0.0)
        attend(kt, vt, jnp.concatenate([mk] * NSA_HPG, axis=0) + lastbias_ref[...] * is_last)

    @pl.when(c == nchunk - 1)
    def _():
        new = new_ref[0]
        attend(new[:LANES], new[LANES:], newbias_ref[...])
        o_ref[0] = acc_sc[0] / l_sc[0] * gates_ref[0][:, 1:2]


def _slc_decode(page_table, cache_t, q32, mrows, new_t, newbias, lastbias, gates32, w, pages_per_step, ppsub):
    nseq, npages = page_table.shape
    pages = min(pages_per_step, npages)
    assert npages % pages == 0 and pages % ppsub == 0
    nchunk = npages // pages
    rows = q32.shape[1]
    dec_rows = rows // NSA_HPG
    seqmap = lambda s, c, pt: (s, 0, 0)
    const = lambda a: pl.BlockSpec(a.shape, lambda s, c, pt: (0,) * a.ndim)
    return pl.pallas_call(
        functools.partial(_slc_decode_kernel, pages=pages, nchunk=nchunk, ppsub=ppsub, dec_rows=dec_rows),
        out_shape=jax.ShapeDtypeStruct((nseq, rows, LANES), F32),
        grid_spec=pltpu.PrefetchScalarGridSpec(
            num_scalar_prefetch=1,
            grid=(nseq, nchunk),
            in_specs=[pl.BlockSpec(memory_space=pl.ANY),
                      pl.BlockSpec((1, rows, LANES), seqmap),
                      pl.BlockSpec((1,) + mrows.shape[1:], seqmap),
                      const(w["blk_expand"]),
                      pl.BlockSpec((1, 2 * LANES, LANES), seqmap),
                      const(newbias), const(lastbias),
                      pl.BlockSpec((1, rows, LANES), seqmap)],
            out_specs=pl.BlockSpec((1, rows, LANES), seqmap),
            scratch_shapes=[pltpu.VMEM((2 * pages * 2 * LANES, LANES), F32),
                            pltpu.SemaphoreType.DMA((2,)),
                            pltpu.VMEM((1, rows, LANES), F32),
                            pltpu.VMEM((1, rows, LANES), F32),
                            pltpu.VMEM((1, rows, LANES), F32),
                            pltpu.VMEM((mrows.shape[1], ppsub * LANES), F32)]),
        name="slc_decode",
        compiler_params=_params(("arbitrary", "arbitrary"), 48),
    )(page_table, cache_t, q32, mrows, w["blk_expand"], new_t, newbias, lastbias, gates32)


def _win_decode_kernel(q_ref, win_ref, new_ref, bias_ref, gates_ref, o_ref, *, dec_rows):
    win = win_ref[0]
    new = new_ref[0]
    kt = jnp.concatenate([win[:LANES], new[:LANES]], axis=1)
    vt = jnp.concatenate([win[LANES:], new[LANES:]], axis=1)
    st = _nn(q_ref[0], kt.astype(BF16))
    sc = _scale_rows(st, _group_rinv(kt, dec_rows), dec_rows) + bias_ref[...]
    m = jnp.max(sc, axis=1, keepdims=True)
    p = jnp.exp(sc - m)
    inv = gates_ref[0][:, 2:3] / jnp.sum(p, axis=1, keepdims=True)
    o_ref[0] = _nt(p.astype(BF16), vt.astype(BF16)) * inv


def _win_decode(q32, win_t, new_t, bias, gates32):
    nseq, rows, _ = q32.shape
    wbuf = win_t.shape[2]
    seqmap = lambda s: (s, 0, 0)
    return pl.pallas_call(
        functools.partial(_win_decode_kernel, dec_rows=rows // NSA_HPG),
        out_shape=jax.ShapeDtypeStruct((nseq, rows, LANES), F32),
        grid=(nseq,),
        in_specs=[pl.BlockSpec((1, rows, LANES), seqmap),
                  pl.BlockSpec((1, 2 * LANES, wbuf), seqmap),
                  pl.BlockSpec((1, 2 * LANES, LANES), seqmap),
                  pl.BlockSpec(bias.shape, lambda s: (0, 0)),
                  pl.BlockSpec((1, rows, LANES), seqmap)],
        out_specs=pl.BlockSpec((1, rows, LANES), seqmap),
        name="win_decode",
        compiler_params=_params(("arbitrary",), 32),
    )(q32, win_t, new_t, bias, gates32)


def _prep_weights(g_attn, w_in, g_q_lat, w_q_b, g_kv_lat, w_ukv, g_mla_q, g_mla_k, w_up_mla,
                  g_nsa_q, g_k_cmp, g_k_slc, g_k_win,
                  pe_cmp_k, w_cmp_k1, b_cmp_k1, w_cmp_k2, pe_cmp_v, w_cmp_v1, b_cmp_v1, w_cmp_v2,
                  w_up_nsa, w_out, g_ffn, w_router_group, b_router_group, w_router_expert, b_router_expert,
                  w_e_gate, w_e_up, w_e_down, *, blocks_per_tile):
    d = w_in.shape[0]
    half_rope = MLA_ROPE // 2
    z = lambda *shape: jnp.zeros(shape, F32)
    row = lambda v: v.reshape(1, -1).astype(F32)
    pad_to = lambda v, n: jnp.concatenate([v, z(v.shape[0], n - v.shape[1])], axis=1)

    def rot(cols):
        return jnp.concatenate([-cols[..., half_rope:], cols[..., :half_rope]], axis=-1)

    offs = np.cumsum([Q_LORA, KV_LORA, MLA_ROPE, NSA_HEADS * NSA_DH, 6 * NSA_GROUPS * NSA_DH, 3 * NSA_HEADS])
    c_q, c_kv, kpe, nq, nkv, ngate, mgate = jnp.split(w_in, offs.tolist(), axis=1)
    seg_a = jnp.concatenate([z(d, MLA_NOPE), kpe, z(d, LANES - MLA_QK)], axis=1)
    seg_b = jnp.concatenate([z(d, MLA_NOPE), rot(kpe), z(d, LANES - MLA_QK)], axis=1)
    nq_h = nq.reshape(d, NSA_HEADS, NSA_DH)
    nq_pp = jnp.concatenate([jnp.concatenate([nq_h[:, j], nq_h[:, j + NSA_HPG]], axis=1)
                             for j in range(NSA_HPG)], axis=1)
    w_in_p = jnp.concatenate([c_q, c_kv, seg_a, seg_b, nq_pp, nkv, pad_to(ngate, LANES), mgate], axis=1)

    wq = w_q_b.reshape(Q_LORA, MLA_HEADS, MLA_QK)
    zq = z(Q_LORA, MLA_HEADS, LANES - MLA_QK)
    wq_a = jnp.concatenate([wq, zq], axis=2).reshape(Q_LORA, MLA_HEADS * LANES)
    wq_b = jnp.concatenate([z(Q_LORA, MLA_HEADS, MLA_NOPE), rot(wq[..., MLA_NOPE:]), zq],
                           axis=2).reshape(Q_LORA, MLA_HEADS * LANES)
    wkv = w_ukv.reshape(KV_LORA, MLA_HEADS, MLA_NOPE + MLA_VDIM)
    wk = wkv[..., :MLA_NOPE]
    wv = wkv[..., MLA_NOPE:]
    wk_p = jnp.concatenate([wk, z(KV_LORA, MLA_HEADS, LANES - MLA_NOPE)], axis=2).reshape(KV_LORA, MLA_HEADS * LANES)
    g96 = lambda g: jnp.concatenate([g, z(LANES - MLA_QK)]).reshape(1, LANES)
    pair = lambda g: jnp.concatenate([g, g]).reshape(1, LANES).astype(F32)

    eye_pe = jnp.eye(MLA_ROPE, dtype=F32)
    w_abs = []
    for hd in range(MLA_HEADS):
        top = jnp.concatenate([wk[:, hd, :].T, z(MLA_NOPE, LANES)], axis=1)
        mid = jnp.concatenate([z(MLA_ROPE, KV_LORA), eye_pe, z(MLA_ROPE, LANES - MLA_ROPE)], axis=1)
        w_abs.append(jnp.concatenate([top, mid, z(LANES - MLA_QK, 2 * LANES)], axis=0))
    w_abs = jnp.stack(w_abs)
    wkt = jnp.concatenate([wk.reshape(KV_LORA, MLA_HEADS * MLA_NOPE).T, z(MLA_HEADS * MLA_NOPE, MLA_ROPE)], axis=1)
    w_kt_aug = jnp.concatenate([wkt, jnp.concatenate([z(MLA_ROPE, KV_LORA), eye_pe], axis=1)], axis=0)
    w_v_pair = []
    for jp in range(MLA_HEADS // 2):
        a = jnp.concatenate([wv[:, 2 * jp, :], z(KV_LORA, MLA_VDIM)], axis=1)
        b = jnp.concatenate([z(KV_LORA, MLA_VDIM), wv[:, 2 * jp + 1, :]], axis=1)
        w_v_pair.append(jnp.concatenate([a, b], axis=0))
    w_v_pair = jnp.stack(w_v_pair)

    def cmp_t(w1):
        return jnp.transpose(w1.reshape(CMP_BLOCK, NSA_DH, CMP_HID), (0, 2, 1))
    wk1t, wv1t = cmp_t(w_cmp_k1), cmp_t(w_cmp_v1)
    zz = z(CMP_BLOCK, CMP_HID, NSA_DH)
    cmp_wt = jnp.concatenate([
        jnp.concatenate([wk1t, zz, zz, zz], axis=2),
        jnp.concatenate([zz, wk1t, zz, zz], axis=2),
        jnp.concatenate([zz, zz, wv1t, zz], axis=2),
        jnp.concatenate([zz, zz, zz, wv1t], axis=2)], axis=1)
    cmp_pe = jnp.concatenate([pe_cmp_k, pe_cmp_k, pe_cmp_v, pe_cmp_v], axis=1).reshape(CMP_BLOCK, 1, 2 * LANES)
    cmp_b1 = jnp.concatenate([b_cmp_k1, b_cmp_k1, b_cmp_v1, b_cmp_v1]).reshape(4 * CMP_HID, 1)

    def w2_bd(w2):
        zt = z(NSA_DH, CMP_HID)
        return jnp.concatenate([jnp.concatenate([w2.T, zt], axis=1), jnp.concatenate([zt, w2.T], axis=1)], axis=0)

    def w1_pages(w1):
        wd = jnp.transpose(w1.reshape(CMP_BLOCK, NSA_DH, CMP_HID), (1, 0, 2))
        zero = jnp.zeros_like(wd)
        both = jnp.stack([jnp.concatenate([wd, zero], axis=2), jnp.concatenate([zero, wd], axis=2)], axis=1)
        return both.reshape(NSA_DH * 2 * CMP_BLOCK, 2 * CMP_HID)

    def pe_pages(pe):
        return jnp.broadcast_to(pe.T[:, None, :], (NSA_DH, 2, CMP_BLOCK)).reshape(1, NSA_DH * 2 * CMP_BLOCK)

    def w2_pages(w2):
        zt = z(CMP_HID, NSA_DH)
        return jnp.concatenate([jnp.concatenate([w2, zt], axis=1), jnp.concatenate([zt, w2], axis=1)], axis=0)

    perm = np.array([HALF * (jp + NSA_HPG * half) + dd for jp in range(NSA_HPG)
                     for half in range(NSA_GROUPS) for dd in range(NSA_DH)])
    w_router = pad_to(jnp.concatenate([w_router_expert, w_router_group], axis=1), LANES)
    b_router = pad_to(jnp.concatenate([b_router_expert, b_router_group]).reshape(1, -1), LANES)

    blk_expand = np.zeros((LANES, blocks_per_tile * CMP_BLOCK), np.float32)
    for i in range(blocks_per_tile):
        blk_expand[i, CMP_BLOCK * i:CMP_BLOCK * (i + 1)] = 1.0

    return dict(
        g_attn=row(g_attn), w_in=w_in_p.astype(BF16), g_q_lat=row(g_q_lat),
        w_q=jnp.concatenate([wq_a, wq_b], axis=1).astype(BF16), g_kv_lat=row(g_kv_lat),
        w_k=wk_p.astype(BF16), w_v=wv.reshape(KV_LORA, MLA_HEADS * MLA_VDIM).astype(BF16),
        g_mla_q=g96(g_mla_q), g_mla_k=g96(g_mla_k), g_nsa_q=pair(g_nsa_q), g_k_slc=pair(g_k_slc),
        g_k_win=pair(g_k_win), g_k_cmp=pair(g_k_cmp),
        g_k_cmp_col=jnp.concatenate([g_k_cmp, g_k_cmp]).reshape(LANES, 1).astype(F32),
        w_abs=w_abs.astype(BF16), w_kt_aug=w_kt_aug.astype(BF16), w_v_pair=w_v_pair.astype(BF16),
        cmp_wt=cmp_wt.astype(BF16), cmp_pe=cmp_pe.astype(F32), cmp_b1=cmp_b1.astype(F32),
        cmp_w2k=w2_bd(w_cmp_k2).astype(BF16), cmp_w2v=w2_bd(w_cmp_v2).astype(BF16),
        cmp_w1k_t=w1_pages(w_cmp_k1).astype(BF16), cmp_w1v_t=w1_pages(w_cmp_v1).astype(BF16),
        cmp_pe_t=jnp.stack([pe_pages(pe_cmp_k), pe_pages(pe_cmp_v)]).astype(F32),
        cmp_b1_t=jnp.stack([jnp.concatenate([b_cmp_k1, b_cmp_k1]).reshape(1, -1),
                            jnp.concatenate([b_cmp_v1, b_cmp_v1]).reshape(1, -1)]).astype(F32),
        cmp_w2k_t=w2_pages(w_cmp_k2).astype(BF16), cmp_w2v_t=w2_pages(w_cmp_v2).astype(BF16),
        w_up_mla=w_up_mla.astype(BF16), w_up_nsa=w_up_nsa[perm].astype(BF16), w_out=w_out.astype(BF16),
        g_ffn=row(g_ffn), w_router=w_router.astype(BF16), b_router=b_router.astype(F32),
        w_gu=jnp.concatenate([w_e_gate, w_e_up], axis=2).astype(BF16), w_down=w_e_down.astype(BF16),
        blk_expand=jnp.asarray(blk_expand, BF16),
    )


def _rope_table(pos):
    half = MLA_ROPE // 2
    inv = ROPE_BASE ** (-jnp.arange(half, dtype=F32) / half)
    ang = pos.astype(F32)[:, None] * inv
    cos, sin = jnp.cos(ang), jnp.sin(ang)
    t = pos.shape[0]
    one, zero = jnp.ones((t, MLA_NOPE), F32), jnp.zeros((t, MLA_NOPE), F32)
    zpad = jnp.zeros((t, LANES - MLA_QK), F32)
    return jnp.concatenate([one, cos, cos, zpad, zero, sin, sin, zpad], axis=1)


TM_PROJ = 256
TQ_MLA = 1024
TQ_CMP = 512
TQ_SLC = 1024
TQ_WIN = 256
TM_MERGE = 256
TM_MOE = 1024
MOE_EXPERTS_PER_STEP = 2
CMP_PAGES_PROMPT = 64
CMP_PAGES = 64
DEC_PAGES = 32
DEC_PPSUB = 32


def kernel(x_prompt, x_sample, cache_mla, cache_nsa_kv, state_nsa_win, page_table, g_attn, w_in, g_q_lat, w_q_b,
           g_kv_lat, w_ukv, g_mla_q, g_mla_k, w_up_mla, g_nsa_q, g_k_cmp, g_k_slc, g_k_win, pe_cmp_k, w_cmp_k1,
           b_cmp_k1, w_cmp_k2, pe_cmp_v, w_cmp_v1, b_cmp_v1, w_cmp_v2, rel_bias, w_up_nsa, w_out, g_ffn,
           w_router_group, b_router_group, w_router_expert, b_router_expert, w_e_gate, w_e_up, w_e_down):
    assert w_in.shape[0] == 1, "single-layer step"
    b, s, d = x_prompt.shape
    nseq, dec_seq, _ = x_sample.shape
    n_pool, page_size = cache_mla.shape[1], cache_mla.shape[2]
    npages = page_table.shape[1]
    past = npages * page_size
    wbuf = state_nsa_win.shape[2]
    nbs = past // CMP_BLOCK
    dec_pages = min(DEC_PAGES, npages)
    ppsub = min(DEC_PPSUB, dec_pages)
    sub_keys = ppsub * page_size
    bps = sub_keys // CMP_BLOCK
    assert page_size == LANES == 2 * CMP_BLOCK and dec_seq < CMP_BLOCK - 1 and s % CMP_BLOCK == 0
    assert past % sub_keys == 0 and wbuf % LANES == 0

    w = _prep_weights(g_attn[0], w_in[0], g_q_lat[0], w_q_b[0], g_kv_lat[0], w_ukv[0], g_mla_q[0], g_mla_k[0],
                      w_up_mla[0], g_nsa_q[0], g_k_cmp[0], g_k_slc[0], g_k_win[0],
                      pe_cmp_k[0], w_cmp_k1[0], b_cmp_k1[0], w_cmp_k2[0], pe_cmp_v[0], w_cmp_v1[0], b_cmp_v1[0],
                      w_cmp_v2[0], w_up_nsa[0], w_out[0], g_ffn[0], w_router_group[0], b_router_group[0],
                      w_router_expert[0], b_router_expert[0], w_e_gate[0], w_e_up[0], w_e_down[0],
                      blocks_per_tile=bps)
    tq_win = min(TQ_WIN, s)
    bias = _bias_tables(rel_bias, tq_win, past, dec_seq, wbuf, nbs, sub_keys)

    tp = b * s
    pp = _project(x_prompt.reshape(tp, d), _rope_table(jnp.arange(s, dtype=jnp.int32)), w, TM_PROJ, False, seq_len=s)
    r3 = lambda a: a.reshape(b, s, a.shape[-1])
    o_mla_p = _mla_prefill(r3(pp["qm"]), r3(pp["km"]), r3(pp["vm"]), TQ_MLA)

    ident = jnp.arange(tp // page_size, dtype=jnp.int32).reshape(b, s // page_size)
    kct_p, vct_p = _compress(ident, pp["nsa4"].reshape(tp // page_size, page_size, 4 * LANES), w, CMP_PAGES_PROMPT)
    tq_cmp = min(TQ_CMP, s)
    ridx = np.arange(tq_cmp) % CMP_BLOCK
    f = bias["f"]
    cols = jnp.concatenate([f[:, 1 + ridx].T, f[:, 1 + CMP_BLOCK + ridx].T,
                            jnp.broadcast_to(f[:, 0][None, :], (tq_cmp, NSA_HEADS)),
                            jnp.zeros((tq_cmp, LANES - 3 * NSA_HEADS), F32)], axis=1)
    gates_p = r3(pp["gates"])
    o_cmp_p, selb_p = _cmp_prefill(r3(pp["nq"]), kct_p, vct_p, cols, gates_p, TQ_CMP)
    o_slc_p = _slc_prefill(r3(pp["nq"]), selb_p, r3(pp["slck"]), r3(pp["slcv"]), bias["td"], bias["ts"],
                           gates_p, TQ_SLC)
    o_win_p = _win_prefill(r3(pp["nq"]), r3(pp["wink"]), r3(pp["winv"]), bias["win"], gates_p, tq_win)
    flat = lambda a: a.reshape(tp, a.shape[-1])
    x1_p, h2_p, comb_p = _merge(x_prompt.reshape(tp, d), flat(o_mla_p), flat(o_cmp_p), flat(o_slc_p),
                                flat(o_win_p), pp["mg"], w, TM_MERGE)
    y_p = _moe(h2_p, x1_p, comb_p, w, TM_MOE, MOE_EXPERTS_PER_STEP).reshape(b, s, d)

    cache_mla_t = jnp.transpose(cache_mla[0], (0, 2, 1))
    cache_nsa_t = jnp.transpose(cache_nsa_kv[0], (0, 2, 3, 4, 1)).reshape(n_pool, 4 * LANES, page_size)
    win_t = jnp.transpose(state_nsa_win[0], (0, 2, 3, 4, 1)).reshape(nseq, 2 * LANES, wbuf)

    ts_ = nseq * dec_seq
    pos_s = past + jnp.arange(dec_seq, dtype=jnp.int32)
    ps = _project(x_sample.reshape(ts_, d), jnp.tile(_rope_table(pos_s), (nseq, 1)), w, TM_PROJ, True)

    def new_cols(a):
        a = jnp.transpose(a.reshape(nseq, dec_seq, a.shape[-1]), (0, 2, 1))
        return jnp.concatenate([a, jnp.zeros((nseq, a.shape[1], LANES - dec_seq), F32)], axis=2)

    rows_h = MLA_HEADS * dec_seq
    qabs = _mla_qabs(ps["qmf"], w).reshape(nseq, rows_h, 2 * LANES)
    tok_r = np.repeat(np.arange(dec_seq), MLA_HEADS)[:, None]
    ncol = np.arange(LANES)[None, :]
    newmask = jnp.asarray(np.where((ncol <= tok_r) & (ncol < dec_seq), 0.0, NEG).astype(np.float32))
    olat = _mla_decode(page_table, cache_mla_t, qabs, new_cols(ps["rows"]), newmask, w, dec_pages, ppsub, dec_seq)
    o_mla_s = _mla_vout(olat.reshape(ts_, MLA_HEADS * LANES), w)

    rows_n = NSA_HEADS * dec_seq
    dec_rows = NSA_GROUPS * dec_seq
    halfmask = jnp.asarray(np.stack([np.arange(LANES) < HALF, np.arange(LANES) >= HALF]).astype(np.float32), BF16)

    def q_rows(nq_pp):
        x = nq_pp.reshape(nseq, dec_seq, NSA_HPG, LANES)
        x = jnp.transpose(x, (0, 2, 1, 3))[:, :, None, :, :] * halfmask[None, None, :, None, :]
        return x.reshape(nseq, rows_n, LANES)

    gsig = ps["gates"].reshape(nseq, dec_seq, LANES)[:, :, :3 * NSA_HEADS]
    gsig = gsig.reshape(nseq, dec_seq, 3, NSA_GROUPS, NSA_HPG)
    gates32 = jnp.transpose(gsig, (0, 4, 3, 1, 2)).reshape(nseq, rows_n, 3)
    gates32 = jnp.concatenate([gates32, jnp.zeros((nseq, rows_n, LANES - 3), F32)], axis=2)

    kc_s, vc_s = _compress_t(page_table, cache_nsa_t, w, CMP_PAGES)
    cand = np.repeat((np.arange(nbs)[None, :] < ((past + np.arange(dec_seq)) // CMP_BLOCK)[:, None]
                      ).astype(np.float32)[None], NSA_GROUPS, axis=0).reshape(dec_rows, nbs)
    o_cmp32, selb8 = _cmp_decode(q_rows(ps["nq"]), kc_s, vc_s, _sample_rows(bias["s_cmp"]),
                                 jnp.asarray(cand), gates32)
    nsub_total = past // sub_keys
    mrows = jnp.transpose(selb8.reshape(nseq, dec_rows, nsub_total, bps), (0, 2, 1, 3))
    mrows = jnp.concatenate([mrows.reshape(nseq, nsub_total * dec_rows, bps),
                             jnp.zeros((nseq, nsub_total * dec_rows, LANES - bps), BF16)], axis=2)
    o_slc32 = _slc_decode(page_table, cache_nsa_t, q_rows(ps["nq_slc"]), mrows, new_cols(ps["nsa4"][:, 2 * LANES:]),
                          _sample_rows(bias["s_new"]), _sample_rows(bias["s_last"]), gates32, w, dec_pages, ppsub)
    new_win_t = new_cols(ps["win2"])
    o_win32 = _win_decode(q_rows(ps["nq_win"]), win_t, new_win_t, _sample_rows(bias["s_win"]), gates32)

    def from_rows(o32):
        x = o32.reshape(nseq, NSA_HPG, NSA_GROUPS, dec_seq, NSA_GROUPS, HALF)
        x = jnp.stack([x[:, :, 0, :, 0, :], x[:, :, 1, :, 1, :]], axis=3)
        return jnp.transpose(x, (0, 2, 1, 3, 4)).reshape(ts_, 4 * LANES)

    x1_s, h2_s, comb_s = _merge(x_sample.reshape(ts_, d), o_mla_s, from_rows(o_cmp32), from_rows(o_slc32),
                                from_rows(o_win32), ps["mg"], w, TM_MERGE)
    y_s = _moe(h2_s, x1_s, comb_s, w, TM_MOE, MOE_EXPERTS_PER_STEP).reshape(nseq, dec_seq, d)

    win_keep = min(WINDOW, s)
    new_mla_prompt = pp["rows"].reshape(1, b, s, MLA_ROW)
    new_mla_sample = ps["rows"].reshape(1, nseq, dec_seq, MLA_ROW)
    new_nsa_prompt = jnp.transpose(pp["nsa4_t"].reshape(b, 4, NSA_GROUPS, NSA_DH, s), (0, 4, 1, 2, 3))[None]
    new_nsa_sample = ps["nsa4"].reshape(1, nseq, dec_seq, 4, NSA_GROUPS, NSA_DH)
    new_win_prompt = pp["win2"].reshape(b, s, 2, NSA_GROUPS, NSA_DH)[None, :, s - win_keep:]
    nw_t = jnp.concatenate([win_t[:, :, dec_seq:], new_win_t[:, :, :dec_seq]], axis=2)
    new_win_sample = jnp.transpose(nw_t.reshape(nseq, 2, NSA_GROUPS, NSA_DH, wbuf), (0, 4, 1, 2, 3))[None]
    return (y_p, y_s, new_mla_prompt, new_mla_sample, new_nsa_prompt, new_nsa_sample, new_win_prompt,
            new_win_sample)
```
